```python
import numpy as np
import jax, jax.numpy as jnp
from jax import lax

D_MODEL = 1024
BATCH = 16
SEQ = 2048
DEPTH = 2

N_A_LAYERS = DEPTH // 2
N_B_LAYERS = DEPTH - N_A_LAYERS
EPS = 1e-6
NEG_INF = -1e30
ROPE_THETA = 10000.0

GLA_HEADS = 4
GLA_DK = D_MODEL // (2 * GLA_HEADS)
GLA_DV = D_MODEL // GLA_HEADS
GLA_GATE_RANK = 16
GLA_TAU = 16.0
GLA_CHUNK = 64
GLA_IN = 2 * GLA_HEADS * GLA_DK + 2 * GLA_HEADS * GLA_DV + GLA_GATE_RANK

NSA_HEADS = 16
NSA_KV_HEADS = 4
NSA_GROUP = NSA_HEADS // NSA_KV_HEADS
NSA_HEAD_DIM = D_MODEL // NSA_HEADS
CMP_BLOCK = 32
CMP_STRIDE = 16
CMP_HIDDEN = 4 * NSA_HEAD_DIM
SEL_BLOCK = 64
SEL_TOPK = 16
WINDOW = 512
NSA_QCHUNK = 32
NSA_IN = NSA_HEADS * NSA_HEAD_DIM + 3 * NSA_HEADS
NSA_KV_OUT = 6 * NSA_KV_HEADS * NSA_HEAD_DIM
FORCE_SCORE = 1e4

FFN_DIM = 2816
CONV_WIDTH = 3

kernel_name = 'yoco_gla_nsa_convffn_trunk'


def rmsnorm(x, g):
    x32 = x.astype(jnp.float32)
    y = x32 * lax.rsqrt(jnp.mean(x32 * x32, axis=-1, keepdims=True) + EPS)
    return (y * g).astype(x.dtype)


def rope(x, pos):
    half = x.shape[-1] // 2
    inv = ROPE_THETA ** (-jnp.arange(half, dtype=jnp.float32) / half)
    ang = pos.astype(jnp.float32)[:, None] * inv[None, :]
    cos, sin = jnp.cos(ang), jnp.sin(ang)
    x1 = x[..., :half].astype(jnp.float32)
    x2 = x[..., half:].astype(jnp.float32)
    return jnp.concatenate([x1 * cos - x2 * sin, x1 * sin + x2 * cos], axis=-1).astype(x.dtype)


def masked_softmax(s, mask):
    s = jnp.where(mask, s.astype(jnp.float32), NEG_INF)
    return jnp.where(mask, jax.nn.softmax(s, axis=-1), 0.0)


def conv_ffn(h, w_up, conv_w, conv_b, w_down):
    S = h.shape[1]
    u = h @ w_up
    up = jnp.pad(u, ((0, 0), (CONV_WIDTH - 1, 0), (0, 0)))
    u = conv_b + up[:, 0:S] * conv_w[0]
    for j in range(1, CONV_WIDTH):
        u = u + up[:, j:j + S] * conv_w[j]
    gate, val = jnp.split(u, 2, axis=-1)
    return (jax.nn.silu(gate) * val) @ w_down


def gla_mixer(h, w_in, w_alpha_up, b_alpha, g_out, w_o):
    Bsz, S, _ = h.shape
    H, DK, DV, C = GLA_HEADS, GLA_DK, GLA_DV, GLA_CHUNK
    NC = S // C
    proj = h @ w_in
    splits = [H * DK, 2 * H * DK, 2 * H * DK + H * DV, 2 * H * DK + 2 * H * DV]
    q, k, v, r, a_low = jnp.split(proj, splits, axis=-1)
    log_a = jax.nn.log_sigmoid((a_low @ w_alpha_up + b_alpha).astype(jnp.float32)) / GLA_TAU

    def to_chunks(t, d):
        return t.astype(jnp.float32).reshape(Bsz, NC, C, H, d).transpose(1, 0, 3, 2, 4)

    qc = to_chunks(q, DK) * DK ** -0.5
    kc = to_chunks(k, DK)
    vc = to_chunks(v, DV)
    gc = to_chunks(log_a, DK)
    causal = jnp.tril(jnp.ones((C, C), dtype=bool))[:, :, None]

    def step(state, inp):
        q_, k_, v_, g_ = inp
        b = jnp.cumsum(g_, axis=2)
        o_inter = jnp.einsum('bhtk,bhkv->bhtv', q_ * jnp.exp(b), state)
        diff = b[:, :, :, None, :] - b[:, :, None, :, :]
        decay = jnp.where(causal, jnp.exp(jnp.minimum(diff, 0.0)), 0.0)
        att = jnp.einsum('bhtk,bhsk,bhtsk->bhts', q_, k_, decay)
        o_intra = jnp.einsum('bhts,bhsv->bhtv', att, v_)
        b_last = b[:, :, -1:, :]
        new_state = state * jnp.exp(b_last[:, :, 0, :, None]) + jnp.einsum(
            'bhsk,bhsv->bhkv', k_ * jnp.exp(b_last - b), v_)
        return new_state, o_inter + o_intra

    state0 = jnp.zeros((Bsz, H, DK, DV), jnp.float32)
    _, o = lax.scan(step, state0, (qc, kc, vc, gc))
    o = o.transpose(1, 0, 3, 2, 4).reshape(Bsz, S, H, DV)
    o = o * lax.rsqrt(jnp.mean(o * o, axis=-1, keepdims=True) + EPS) * g_out
    o = o.reshape(Bsz, S, H * DV).astype(h.dtype) * jax.nn.silu(r)
    return o @ w_o


def compress_blocks(t_raw, pe, w1, w2):
    Bsz, Hk, S, Dh = t_raw.shape
    n_cmp = (S - CMP_BLOCK) // CMP_STRIDE + 1
    idx = np.arange(n_cmp)[:, None] * CMP_STRIDE + np.arange(CMP_BLOCK)[None, :]
    blocks = (t_raw[:, :, idx, :] + pe).reshape(Bsz, Hk, n_cmp, CMP_BLOCK * Dh)
    return jax.nn.gelu(blocks @ w1) @ w2


def cmp_to_sel_weights(S):
    n_cmp = (S - CMP_BLOCK) // CMP_STRIDE + 1
    n_sel = S // SEL_BLOCK
    c0 = np.arange(n_cmp)[:, None] * CMP_STRIDE
    s0 = np.arange(n_sel)[None, :] * SEL_BLOCK
    ov = np.clip(np.minimum(c0 + CMP_BLOCK, s0 + SEL_BLOCK) - np.maximum(c0, s0), 0, None)
    return (ov / CMP_BLOCK).astype(np.float32)


def nsa_shared_kv(x, g_kv, w_kv, pe_k, pe_v, wk1, wk2, wv1, wv2):
    Bsz, S, _ = x.shape
    hn = rmsnorm(x, g_kv)
    kv = (hn @ w_kv).reshape(Bsz, S, 6, NSA_KV_HEADS, NSA_HEAD_DIM).transpose(2, 0, 3, 1, 4)
    pos = jnp.arange(S)
    k_cmp = compress_blocks(kv[0], pe_k, wk1, wk2)
    v_cmp = compress_blocks(kv[1], pe_v, wv1, wv2)
    k_sel = rope(kv[2], pos)
    k_win = rope(kv[4], pos)
    return k_cmp, v_cmp, k_sel, kv[3], k_win, kv[5]


gather_blocks = jax.vmap(jax.vmap(lambda blocks, ix: blocks[ix]))


def nsa_mixer(h, w_in, w_o, k_cmp, v_cmp, k_sel, v_sel, k_win, v_win):
    Bsz, S, _ = h.shape
    Hk, G, Dh, SB, QC = NSA_KV_HEADS, NSA_GROUP, NSA_HEAD_DIM, SEL_BLOCK, NSA_QCHUNK
    n_cmp = k_cmp.shape[2]
    n_sel = S // SB
    top = min(SEL_TOPK, n_sel)
    scale = Dh ** -0.5
    pos = jnp.arange(S)
    proj = h @ w_in
    q = proj[..., :NSA_HEADS * Dh].reshape(Bsz, S, Hk, G, Dh).transpose(0, 2, 3, 1, 4)
    gates = jax.nn.sigmoid(proj[..., NSA_HEADS * Dh:].astype(jnp.float32))
    gates = gates.reshape(Bsz, S, 3, Hk, G).transpose(2, 0, 3, 4, 1)[..., None]

    cmp_end = jnp.arange(n_cmp) * CMP_STRIDE + CMP_BLOCK - 1
    p_cmp = masked_softmax(jnp.einsum('bkgsd,bknd->bkgsn', q, k_cmp) * scale,
                           cmp_end[None, :] <= pos[:, None])
    o_cmp = jnp.einsum('bkgsn,bknd->bkgsd', p_cmp, v_cmp)

    imp = jnp.einsum('bkgsn,nj->bksj', p_cmp, jnp.asarray(cmp_to_sel_weights(S)))
    blk = jnp.arange(n_sel)[None, :]
    cur = (pos // SB)[:, None]
    forced = (blk == 0) | (blk == cur) | (blk == cur - 1)
    imp = jnp.where(blk > cur, -1.0, jnp.where(forced, FORCE_SCORE, imp))
    _, sel_idx = lax.top_k(imp, top)

    q_rot = rope(q, pos)
    k_blocks = k_sel.reshape(Bsz, Hk, n_sel, SB, Dh)
    v_blocks = v_sel.reshape(Bsz, Hk, n_sel, SB, Dh)
    pad = ((0, 0), (0, 0), (WINDOW, 0), (0, 0))
    k_win_pad = jnp.pad(k_win, pad)
    v_win_pad = jnp.pad(v_win, pad)

    def chunk(c):
        t0 = c * QC
        qc = lax.dynamic_slice_in_dim(q_rot, t0, QC, axis=3)
        qpos = t0 + jnp.arange(QC)
        idx = lax.dynamic_slice_in_dim(sel_idx, t0, QC, axis=2)
        ks = gather_blocks(k_blocks, idx).reshape(Bsz, Hk, QC, top * SB, Dh)
        vs = gather_blocks(v_blocks, idx).reshape(Bsz, Hk, QC, top * SB, Dh)
        kpos = (idx[..., None] * SB + jnp.arange(SB)).reshape(Bsz, Hk, 1, QC, top * SB)
        p_sel = masked_softmax(jnp.einsum('bkgqd,bkqmd->bkgqm', qc, ks) * scale,
                               kpos <= qpos[:, None])
        o_sel = jnp.einsum('bkgqm,bkqmd->bkgqd', p_sel, vs)
        kw = lax.dynamic_slice_in_dim(k_win_pad, t0, WINDOW + QC, axis=2)
        vw = lax.dynamic_slice_in_dim(v_win_pad, t0, WINDOW + QC, axis=2)
        kwpos = t0 - WINDOW + jnp.arange(WINDOW + QC)
        dist = qpos[:, None] - kwpos[None, :]
        win_mask = (kwpos[None, :] >= 0) & (dist >= 0) & (dist < WINDOW)
        p_win = masked_softmax(jnp.einsum('bkgqd,bknd->bkgqn', qc, kw) * scale, win_mask)
        o_win = jnp.einsum('bkgqn,bknd->bkgqd', p_win, vw)
        return o_sel, o_win

    o_sel, o_win = lax.map(chunk, jnp.arange(S // QC))
    o_sel = jnp.moveaxis(o_sel, 0, 3).reshape(Bsz, Hk, G, S, Dh)
    o_win = jnp.moveaxis(o_win, 0, 3).reshape(Bsz, Hk, G, S, Dh)
    o = gates[0] * o_cmp + gates[1] * o_sel + gates[2] * o_win
    o = o.transpose(0, 3, 1, 2, 4).reshape(Bsz, S, NSA_HEADS * Dh).astype(h.dtype)
    return o @ w_o


def setup_inputs(seed: int = 0) -> dict:
    key = jax.random.key(seed)
    ks = jax.random.split(key, 32)
    f32 = jnp.float32

    def nrm(k, shape, fan_in):
        return jax.random.normal(k, shape, f32) * fan_in ** -0.5

    def gain(k, shape):
        return 1.0 + 0.02 * jax.random.normal(k, shape, f32)

    two_f = 2 * FFN_DIM
    return {
        'x': jax.random.normal(ks[0], (BATCH, SEQ, D_MODEL), f32),
        'norm_mix': gain(ks[1], (DEPTH, D_MODEL)),
        'norm_ffn': gain(ks[2], (DEPTH, D_MODEL)),
        'gla_w_in': nrm(ks[3], (N_A_LAYERS, D_MODEL, GLA_IN), D_MODEL),
        'gla_w_alpha_up': nrm(ks[4], (N_A_LAYERS, GLA_GATE_RANK, GLA_HEADS * GLA_DK), GLA_GATE_RANK),
        'gla_b_alpha': 0.1 * jax.random.normal(ks[5], (N_A_LAYERS, GLA_HEADS * GLA_DK), f32),
        'gla_norm': gain(ks[6], (N_A_LAYERS, GLA_HEADS, GLA_DV)),
        'gla_w_o': nrm(ks[7], (N_A_LAYERS, GLA_HEADS * GLA_DV, D_MODEL), GLA_HEADS * GLA_DV),
        'kv_norm': gain(ks[8], (D_MODEL,)),
        'nsa_w_kv': nrm(ks[9], (D_MODEL, NSA_KV_OUT), D_MODEL),
        'cmp_pe_k': 0.02 * jax.random.normal(ks[10], (CMP_BLOCK, NSA_HEAD_DIM), f32),
        'cmp_pe_v': 0.02 * jax.random.normal(ks[11], (CMP_BLOCK, NSA_HEAD_DIM), f32),
        'cmp_k_w1': nrm(ks[12], (CMP_BLOCK * NSA_HEAD_DIM, CMP_HIDDEN), CMP_BLOCK * NSA_HEAD_DIM),
        'cmp_k_w2': nrm(ks[13], (CMP_HIDDEN, NSA_HEAD_DIM), CMP_HIDDEN),
        'cmp_v_w1': nrm(ks[14], (CMP_BLOCK * NSA_HEAD_DIM, CMP_HIDDEN), CMP_BLOCK * NSA_HEAD_DIM),
        'cmp_v_w2': nrm(ks[15], (CMP_HIDDEN, NSA_HEAD_DIM), CMP_HIDDEN),
        'nsa_w_in': nrm(ks[16], (N_B_LAYERS, D_MODEL, NSA_IN), D_MODEL),
        'nsa_w_o': nrm(ks[17], (N_B_LAYERS, NSA_HEADS * NSA_HEAD_DIM, D_MODEL), NSA_HEADS * NSA_HEAD_DIM),
        'ffn_w_up': nrm(ks[18], (DEPTH, D_MODEL, two_f), D_MODEL),
        'ffn_conv_w': nrm(ks[19], (DEPTH, CONV_WIDTH, two_f), CONV_WIDTH),
        'ffn_conv_b': 0.02 * jax.random.normal(ks[20], (DEPTH, two_f), f32),
        'ffn_w_down': nrm(ks[21], (DEPTH, FFN_DIM, D_MODEL), FFN_DIM),
        'norm_final': gain(ks[22], (D_MODEL,)),
    }


def reference(x, norm_mix, norm_ffn, gla_w_in, gla_w_alpha_up, gla_b_alpha, gla_norm, gla_w_o,
              kv_norm, nsa_w_kv, cmp_pe_k, cmp_pe_v, cmp_k_w1, cmp_k_w2, cmp_v_w1, cmp_v_w2,
              nsa_w_in, nsa_w_o, ffn_w_up, ffn_conv_w, ffn_conv_b, ffn_w_down, norm_final):
    shared = None
    for layer in range(DEPTH):
        if layer < N_A_LAYERS:
            hn = rmsnorm(x, norm_mix[layer])
            x = x + gla_mixer(hn, gla_w_in[layer], gla_w_alpha_up[layer], gla_b_alpha[layer],
                              gla_norm[layer], gla_w_o[layer])
        else:
            if layer == N_A_LAYERS:
                shared = nsa_shared_kv(x, kv_norm, nsa_w_kv, cmp_pe_k, cmp_pe_v,
                                       cmp_k_w1, cmp_k_w2, cmp_v_w1, cmp_v_w2)
            b = layer - N_A_LAYERS
            hn = rmsnorm(x, norm_mix[layer])
            x = x + nsa_mixer(hn, nsa_w_in[b], nsa_w_o[b], *shared)
        hn = rmsnorm(x, norm_ffn[layer])
        x = x + conv_ffn(hn, ffn_w_up[layer], ffn_conv_w[layer], ffn_conv_b[layer], ffn_w_down[layer])
    return rmsnorm(x, norm_final)
```

```python
import functools

import numpy as np
import jax
import jax.numpy as jnp
from jax import lax
from jax.experimental import pallas as pl
from jax.experimental.pallas import tpu as pltpu

F32 = jnp.float32
BF16 = jnp.bfloat16

EPS = 1e-6
NEG_INF = -1e30
ROPE_THETA = 10000.0

LANE = 128
VMEM_LIMIT_BYTES = 56 * 1024 * 1024

GLA_HEADS = 4
GLA_DK = 128
GLA_DV = 256
GLA_GATE_RANK = 16
GLA_TAU = 16.0
GLA_CHUNK = 64
NSA_HEADS = 16
NSA_KV_HEADS = 4
NSA_GROUP = 4
NSA_DH = 64
CMP_BLOCK = 32
CMP_STRIDE = 16
CMP_HIDDEN = 256
SEL_BLOCK = 64
SEL_TOPK = 16
WINDOW = 512
FORCE_SCORE = 1e4
CONV_WIDTH = 3
CONV_HALO = 16


def _dot(a, b):
    return jnp.dot(a, b, preferred_element_type=F32)


def _dot_nt(a, b):
    return lax.dot_general(a, b, (((1,), (1,)), ((), ())), preferred_element_type=F32)


def _dot_tn(a, b):
    return lax.dot_general(a, b, (((0,), (0,)), ((), ())), preferred_element_type=F32)


def _rms(x, g):
    return x * lax.rsqrt(jnp.mean(x * x, axis=-1, keepdims=True) + EPS) * g


def _split3(x):
    hi = x.astype(BF16)
    r1 = x - hi.astype(F32)
    mid = r1.astype(BF16)
    lo = (r1 - mid.astype(F32)).astype(BF16)
    return hi, mid, lo


def _rope_tile(x, cos, sin_signed):
    w = x.shape[-1]
    lane = lax.broadcasted_iota(jnp.int32, x.shape, 1)
    first_half = (lane & (NSA_DH - 1)) < (NSA_DH // 2)
    partner = jnp.where(first_half, pltpu.roll(x, w - NSA_DH // 2, 1), pltpu.roll(x, NSA_DH // 2, 1))
    return x * cos + partner * sin_signed


def _params(*sem):
    return pltpu.CompilerParams(dimension_semantics=sem, vmem_limit_bytes=VMEM_LIMIT_BYTES)


def _norm_proj_kernel(*refs, rope_tiles, rope_second, scale, act, has_rope):
    if has_rope:
        x_ref, g_ref, w_ref, cos_ref, sin_ref = refs[:5]
        rest = refs[5:]
    else:
        x_ref, g_ref, w_ref = refs[:3]
        rest = refs[3:]
    if rope_second:
        o_ref, o2_ref, hn_ref = rest
    else:
        o_ref, hn_ref = rest
    j = pl.program_id(1)

    @pl.when(j == 0)
    def _():
        hn_ref[...] = _rms(x_ref[...], g_ref[...]).astype(BF16)

    acc = _dot(hn_ref[...], w_ref[...])
    if scale != 1.0:
        acc = acc * scale
    if act == "sigmoid":
        acc = jax.nn.sigmoid(acc)
    if has_rope:
        roped = _rope_tile(acc, cos_ref[...], sin_ref[...])
        if rope_second:
            o2_ref[...] = roped.astype(o2_ref.dtype)
        else:
            is_rope = functools.reduce(jnp.logical_or, [j == t for t in rope_tiles])
            acc = jnp.where(is_rope, roped, acc)
    o_ref[...] = acc.astype(o_ref.dtype)


def _norm_proj(x2, g, w, *, out_dtype, tm, tn, seq, rope=None, rope_tiles=(), rope_second=False,
               scale=1.0, act=None):
    n, d = x2.shape
    m = w.shape[1]
    has_rope = rope is not None
    in_specs = [
        pl.BlockSpec((tm, d), lambda i, j: (i, 0)),
        pl.BlockSpec((1, d), lambda i, j: (0, 0)),
        pl.BlockSpec((d, tn), lambda i, j: (0, j)),
    ]
    args = [x2, g.reshape(1, d), w]
    if has_rope:
        per_seq = seq // tm
        in_specs += [pl.BlockSpec((tm, tn), lambda i, j: (i % per_seq, 0))] * 2
        args += list(rope)
    out_shape = [jax.ShapeDtypeStruct((n, m), out_dtype)]
    out_specs = [pl.BlockSpec((tm, tn), lambda i, j: (i, j))]
    if rope_second:
        out_shape.append(jax.ShapeDtypeStruct((n, m), out_dtype))
        out_specs.append(pl.BlockSpec((tm, tn), lambda i, j: (i, j)))
    res = pl.pallas_call(
        functools.partial(_norm_proj_kernel, rope_tiles=rope_tiles, rope_second=rope_second,
                          scale=scale, act=act, has_rope=has_rope),
        out_shape=out_shape,
        grid=(n // tm, m // tn),
        in_specs=in_specs,
        out_specs=out_specs,
        scratch_shapes=[pltpu.VMEM((tm, d), BF16)],
        compiler_params=_params("parallel", "arbitrary"),
    )(*args)
    return res if rope_second else res[0]


def _gla_gate_kernel(x_ref, g_ref, wl_ref, wu_ref, b_ref, o_ref):
    hn = _rms(x_ref[...], g_ref[...]).astype(BF16)
    a_low = _dot(hn, wl_ref[...])
    hi, mid, lo = _split3(a_low)
    wu = wu_ref[...]
    z = b_ref[...]
    for a in (hi, mid, lo):
        for p in range(3):
            z = z + _dot(a, wu[p])
    log_sig = jnp.minimum(z, 0.0) - jnp.log(1.0 + jnp.exp(-jnp.abs(z)))
    o_ref[...] = log_sig * (1.0 / GLA_TAU)


def _gla_gate(x2, g, w_low, w_up3, b, *, tm):
    n, d = x2.shape
    hk = w_up3.shape[-1]
    return pl.pallas_call(
        _gla_gate_kernel,
        out_shape=jax.ShapeDtypeStruct((n, hk), F32),
        grid=(n // tm,),
        in_specs=[
            pl.BlockSpec((tm, d), lambda i: (i, 0)),
            pl.BlockSpec((1, d), lambda i: (0, 0)),
            pl.BlockSpec((d, LANE), lambda i: (0, 0)),
            pl.BlockSpec((3, LANE, hk), lambda i: (0, 0, 0)),
            pl.BlockSpec((1, hk), lambda i: (0, 0)),
        ],
        out_specs=pl.BlockSpec((tm, hk), lambda i: (i, 0)),
        compiler_params=_params("parallel"),
    )(x2, g.reshape(1, d), w_low, w_up3, b.reshape(1, hk))


def _gla_constants():
    c = GLA_CHUNK
    idx = np.arange(c)
    row, col = idx[:, None], idx[None, :]
    mats = [col <= row, col > row]
    masks = []
    h = 1
    while h < c:
        blk = idx // (2 * h)
        upper = (idx % (2 * h)) >= h
        r = (blk * 2 * h + h - 1)[:, None]
        mats.append(np.where(upper[:, None], (col > r) & (col <= row), (col > row) & (col <= r)))
        masks.append((blk[:, None] == blk[None, :]) & upper[:, None] & (~upper[None, :]))
        h *= 2
    masks.append(np.eye(c, dtype=bool))
    return (np.concatenate(mats, 0).astype(np.float32), np.stack(masks).astype(np.float32))


def _gla_kernel(q_ref, k_ref, v_ref, r_ref, g_ref, gs_ref, mk_ref, gn_ref, o_ref, st_ref, *, n_chunks):
    c, dk = GLA_CHUNK, GLA_DK
    n_levels = mk_ref.shape[0] - 1
    st_ref[...] = jnp.zeros_like(st_ref)
    gs = gs_ref[...]
    q_scale = dk ** -0.5

    def body(ci, carry):
        r0 = pl.multiple_of(ci * c, c)
        q = q_ref[0, pl.ds(r0, c), :].astype(F32) * q_scale
        k = k_ref[0, pl.ds(r0, c), :].astype(F32)
        v = v_ref[0, pl.ds(r0, c), :]
        hi, mid, lo = _split3(g_ref[0, pl.ds(r0, c), :])
        e3 = _dot(gs, jnp.concatenate([hi, mid, lo], axis=1))
        x = jnp.exp(e3[:, :dk] + e3[:, dk:2 * dk] + e3[:, 2 * dk:])
        att = _dot_nt(q.astype(BF16), k.astype(BF16)) * mk_ref[n_levels]
        for lvl in range(n_levels):
            xl = x[(2 + lvl) * c:(3 + lvl) * c]
            att = att + _dot_nt((q * xl).astype(BF16), (k * xl).astype(BF16)) * mk_ref[lvl]
        o = _dot(att.astype(BF16), v)
        st = st_ref[...]
        o = o + _dot_nt((q * x[0:c]).astype(BF16), st.astype(BF16))
        st_ref[...] = st * x[c - 1:c] + _dot_tn(v, (k * x[c:2 * c]).astype(BF16))
        o = o * lax.rsqrt(jnp.mean(o * o, axis=-1, keepdims=True) + EPS) * gn_ref[0]
        r = r_ref[0, pl.ds(r0, c), :].astype(F32)
        o_ref[0, pl.ds(r0, c), :] = (o * (r * jax.nn.sigmoid(r))).astype(o_ref.dtype)
        return carry

    lax.fori_loop(0, n_chunks, body, 0)


def _gla_core(proj, log_a, gla_norm):
    b, s, _ = proj.shape
    h, dk, dv = GLA_HEADS, GLA_DK, GLA_DV
    gs, mk = _gla_constants()
    qk_blocks = h * dk // dk
    v_base = 2 * h * dk // dv
    return pl.pallas_call(
        functools.partial(_gla_kernel, n_chunks=s // GLA_CHUNK),
        out_shape=jax.ShapeDtypeStruct((b, s, h * dv), BF16),
        grid=(b, h),
        in_specs=[
            pl.BlockSpec((1, s, dk), lambda i, j: (i, 0, j)),
            pl.BlockSpec((1, s, dk), lambda i, j: (i, 0, qk_blocks + j)),
            pl.BlockSpec((1, s, dv), lambda i, j: (i, 0, v_base + j)),
            pl.BlockSpec((1, s, dv), lambda i, j: (i, 0, v_base + h + j)),
            pl.BlockSpec((1, s, dk), lambda i, j: (i, 0, j)),
            pl.BlockSpec(gs.shape, lambda i, j: (0, 0)),
            pl.BlockSpec(mk.shape, lambda i, j: (0, 0, 0)),
            pl.BlockSpec((1, 1, dv), lambda i, j: (j, 0, 0)),
        ],
        out_specs=pl.BlockSpec((1, s, dv), lambda i, j: (i, 0, j)),
        scratch_shapes=[pltpu.VMEM((dv, dk), F32)],
        compiler_params=_params("parallel", "parallel"),
    )(proj, proj, proj, proj, log_a, jnp.asarray(gs, BF16), jnp.asarray(mk), gla_norm.reshape(h, 1, dv))


def _proj_res_kernel(a_ref, w_ref, r_ref, o_ref):
    o_ref[...] = r_ref[...] + _dot(a_ref[...], w_ref[...])


def _proj_res(a, w, res, *, tm):
    n, k = a.shape
    d = w.shape[1]
    return pl.pallas_call(
        _proj_res_kernel,
        out_shape=jax.ShapeDtypeStruct((n, d), F32),
        grid=(n // tm,),
        in_specs=[
            pl.BlockSpec((tm, k), lambda i: (i, 0)),
            pl.BlockSpec((k, d), lambda i: (0, 0)),
            pl.BlockSpec((tm, d), lambda i: (i, 0)),
        ],
        out_specs=pl.BlockSpec((tm, d), lambda i: (i, 0)),
        compiler_params=_params("parallel"),
    )(a, w, res)


def _ffn_kernel(*refs, tm, final):
    if final:
        (xm_ref, xh_ref, gn_ref, wg_ref, wv_ref, cwg_ref, cwv_ref, cbg_ref, cbv_ref, wd_ref, gf_ref,
         o_ref, hn_ref, acc_ref, ug_ref, uv_ref) = refs
    else:
        (xm_ref, xh_ref, gn_ref, wg_ref, wv_ref, cwg_ref, cwv_ref, cbg_ref, cbv_ref, wd_ref,
         o_ref, hn_ref, acc_ref, ug_ref, uv_ref) = refs
    halo = CONV_HALO
    i = pl.program_id(1)
    f = pl.program_id(2)

    @pl.when(f == 0)
    def _():
        g = gn_ref[...]
        hn_ref[0:halo, :] = _rms(xh_ref[0], g).astype(BF16)
        hn_ref[halo:, :] = _rms(xm_ref[0], g).astype(BF16)
        acc_ref[...] = jnp.zeros_like(acc_ref)

    hn = hn_ref[...]
    row = lax.broadcasted_iota(jnp.int32, (tm + halo, 1), 0)
    keep = jnp.where((row >= halo) | (i > 0), 1.0, 0.0)
    ug_ref[...] = _dot(hn, wg_ref[...]) * keep
    uv_ref[...] = _dot(hn, wv_ref[...]) * keep

    def conv(u_ref, cw_ref, cb_ref):
        out = cb_ref[...]
        for tap in range(CONV_WIDTH):
            start = halo - (CONV_WIDTH - 1) + tap
            out = out + u_ref[start:start + tm, :] * cw_ref[tap:tap + 1, :]
        return out

    gate = conv(ug_ref, cwg_ref, cbg_ref)
    val = conv(uv_ref, cwv_ref, cbv_ref)
    act = (gate * jax.nn.sigmoid(gate) * val).astype(BF16)
    acc_ref[...] += _dot(act, wd_ref[...])

    @pl.when(f == pl.num_programs(2) - 1)
    def _():
        y = xm_ref[0] + acc_ref[...]
        if final:
            y = _rms(y, gf_ref[...])
        o_ref[0] = y


def _conv_ffn(x3, g, w_up, conv_w, conv_b, w_down, final_gain=None, *, tm, tf):
    b, s, d = x3.shape
    ffn = w_down.shape[0]
    nf = ffn // tf
    halo = CONV_HALO
    final = final_gain is not None
    in_specs = [
        pl.BlockSpec((1, tm, d), lambda bi, i, f: (bi, i, 0)),
        pl.BlockSpec((1, halo, d), lambda bi, i, f: (bi, jnp.maximum(i * (tm // halo) - 1, 0), 0)),
        pl.BlockSpec((1, d), lambda bi, i, f: (0, 0)),
        pl.BlockSpec((d, tf), lambda bi, i, f: (0, f)),
        pl.BlockSpec((d, tf), lambda bi, i, f: (0, nf + f)),
        pl.BlockSpec((CONV_WIDTH, tf), lambda bi, i, f: (0, f)),
        pl.BlockSpec((CONV_WIDTH, tf), lambda bi, i, f: (0, nf + f)),
        pl.BlockSpec((1, tf), lambda bi, i, f: (0, f)),
        pl.BlockSpec((1, tf), lambda bi, i, f: (0, nf + f)),
        pl.BlockSpec((tf, d), lambda bi, i, f: (f, 0)),
    ]
    args = [x3, x3, g.reshape(1, d), w_up, w_up, conv_w, conv_w, conv_b.reshape(1, -1),
            conv_b.reshape(1, -1), w_down]
    if final:
        in_specs.append(pl.BlockSpec((1, d), lambda bi, i, f: (0, 0)))
        args.append(final_gain.reshape(1, d))
    return pl.pallas_call(
        functools.partial(_ffn_kernel, tm=tm, final=final),
        out_shape=jax.ShapeDtypeStruct((b, s, d), F32),
        grid=(b, s // tm, nf),
        in_specs=in_specs,
        out_specs=pl.BlockSpec((1, tm, d), lambda bi, i, f: (bi, i, 0)),
        scratch_shapes=[
            pltpu.VMEM((tm + halo, d), BF16),
            pltpu.VMEM((tm, d), F32),
            pltpu.VMEM((tm + halo, tf), F32),
            pltpu.VMEM((tm + halo, tf), F32),
        ],
        compiler_params=_params("parallel", "parallel", "arbitrary"),
    )(*args)


def _compress_kernel(tk_ref, tv_ref, w1k_ref, w1v_ref, bk_ref, bv_ref, w2k_ref, w2v_ref, o_ref, *, rows_per_seq):
    rows = tk_ref.shape[0]
    row = lax.broadcasted_iota(jnp.int32, (rows, 1), 0)
    valid = jnp.where((row & (rows_per_seq - 1)) < rows_per_seq - 1, 1.0, 0.0)

    def one(t_ref, w1_ref, b_ref, w2_ref):
        y = _dot(t_ref[...], w1_ref[...])
        hid = y.shape[1] // 2
        pre = y[:, :hid] + pltpu.roll(y[:, hid:], rows - 1, 0) + b_ref[0:1, :]
        out = _dot(jax.nn.gelu(pre).astype(BF16), w2_ref[...])
        return out * valid

    o_ref[...] = jnp.concatenate(
        [one(tk_ref, w1k_ref, bk_ref, w2k_ref), one(tv_ref, w1v_ref, bv_ref, w2v_ref)], axis=1
    ).astype(o_ref.dtype)


def _pe_bias_kernel(pk_ref, pv_ref, w1k_ref, w1v_ref, ok_ref, ov_ref):
    ok_ref[...] = _dot(pk_ref[...], w1k_ref[...])
    ov_ref[...] = _dot(pv_ref[...], w1v_ref[...])


def _compress(tk, tv, pe_k, pe_v, w1k, w1v, w2k, w2v, *, rows_per_seq, tr):
    r, kdim = tk.shape
    hid = w1k.shape[1]
    dh = w2k.shape[1]
    sub = 8
    pk = jnp.broadcast_to(pe_k.reshape(1, -1), (sub, 2 * kdim)).astype(BF16)
    pv = jnp.broadcast_to(pe_v.reshape(1, -1), (sub, 2 * kdim)).astype(BF16)
    bk, bv = pl.pallas_call(
        _pe_bias_kernel,
        out_shape=[jax.ShapeDtypeStruct((sub, hid), F32)] * 2,
    )(pk, pv, w1k, w1v)
    w1k_cat = jnp.concatenate([w1k[:kdim], w1k[kdim:]], axis=1)
    w1v_cat = jnp.concatenate([w1v[:kdim], w1v[kdim:]], axis=1)
    return pl.pallas_call(
        functools.partial(_compress_kernel, rows_per_seq=rows_per_seq),
        out_shape=jax.ShapeDtypeStruct((r, 2 * dh), BF16),
        grid=(r // tr,),
        in_specs=[
            pl.BlockSpec((tr, kdim), lambda i: (i, 0)),
            pl.BlockSpec((tr, kdim), lambda i: (i, 0)),
            pl.BlockSpec((kdim, 2 * hid), lambda i: (0, 0)),
            pl.BlockSpec((kdim, 2 * hid), lambda i: (0, 0)),
            pl.BlockSpec((sub, hid), lambda i: (0, 0)),
            pl.BlockSpec((sub, hid), lambda i: (0, 0)),
            pl.BlockSpec((hid, dh), lambda i: (0, 0)),
            pl.BlockSpec((hid, dh), lambda i: (0, 0)),
        ],
        out_specs=pl.BlockSpec((tr, 2 * dh), lambda i: (i, 0)),
        compiler_params=_params("parallel"),
    )(tk, tv, w1k_cat, w1v_cat, bk, bv, w2k, w2v)


def _cmp_to_sel_weights_t(n_cmp_pad, n_sel):
    c0 = np.arange(n_cmp_pad)[None, :] * CMP_STRIDE
    s0 = np.arange(n_sel)[:, None] * SEL_BLOCK
    ov = np.clip(np.minimum(c0 + CMP_BLOCK, s0 + SEL_BLOCK) - np.maximum(c0, s0), 0, None)
    return (ov / CMP_BLOCK).astype(np.float32)


def _nsa_kernel(q_ref, qr_ref, gt_ref, kc_ref, ks_ref, vs_ref, kw_ref, vw_ref, wt_ref, o_ref, acc_ref,
                *, tq, tk):
    grp, dh = NSA_GROUP, NSA_DH
    gt = grp * tq
    qi = pl.program_id(2)
    t0 = qi * tq
    n_sel = wt_ref.shape[0]
    n_cmp_pad = wt_ref.shape[1]
    blocks_per_tile = tk // SEL_BLOCK

    lane = lax.broadcasted_iota(jnp.int32, (1, gt), 1)
    tpos = t0 + (lane & (tq - 1))
    tpos_q = t0 + lax.broadcasted_iota(jnp.int32, (1, tq), 1)
    gates_t = gt_ref[0].T
    key_iota = lax.broadcasted_iota(jnp.int32, (tk, 1), 0)

    for h2 in range(2):
        def heads_to_rows(ref):
            return jnp.concatenate(
                [ref[0, :, (h2 * grp + g) * dh:(h2 * grp + g + 1) * dh] for g in range(grp)], axis=0)

        q = heads_to_rows(q_ref)
        q_rot = heads_to_rows(qr_ref)
        k_cmp = kc_ref[0, h2, :, 0:dh]
        v_cmp = kc_ref[0, h2, :, dh:2 * dh]

        s = _dot_nt(k_cmp, q)
        cmp_end = lax.broadcasted_iota(jnp.int32, (n_cmp_pad, 1), 0) * CMP_STRIDE + (CMP_BLOCK - 1)
        mask = cmp_end <= tpos
        s = jnp.where(mask, s, NEG_INF)
        p = jnp.exp(s - jnp.max(s, axis=0, keepdims=True))
        p = p / jnp.sum(p, axis=0, keepdims=True)
        p = jnp.where(mask, p, 0.0)
        o_cmp = _dot_tn(v_cmp, p.astype(BF16))

        p_grp = p[:, 0:tq]
        for g in range(1, grp):
            p_grp = p_grp + p[:, g * tq:(g + 1) * tq]
        i3 = _dot(wt_ref[...], jnp.concatenate(_split3(p_grp), axis=1))
        imp = i3[:, 0:tq] + i3[:, tq:2 * tq] + i3[:, 2 * tq:]
        blk = lax.broadcasted_iota(jnp.int32, (n_sel, 1), 0)
        cur = tpos_q >> 6
        forced = (blk == 0) | (blk == cur) | (blk == cur - 1)
        imp = jnp.where(blk > cur, -1.0, jnp.where(forced, FORCE_SCORE, imp))
        rank = jnp.zeros((n_sel, tq), F32)
        for jp in range(n_sel):
            other = imp[jp:jp + 1, :]
            ge = jnp.where(other >= imp, 1.0, 0.0)
            gt_ = jnp.where(other > imp, 1.0, 0.0)
            rank = rank + jnp.where(blk > jp, ge, gt_)
        sel = jnp.where(rank < float(SEL_TOPK), 1.0, 0.0).astype(BF16)

        def attend(k_ref, v_ref, kt, mask_fn, carry):
            m, l = carry
            k0 = pl.multiple_of(kt * tk, tk)
            k = k_ref[0, pl.ds(k0, tk), h2 * dh:(h2 + 1) * dh]
            v = v_ref[0, pl.ds(k0, tk), h2 * dh:(h2 + 1) * dh]
            s = _dot_nt(k, q_rot)
            s = jnp.where(mask_fn(kt, k0 + key_iota), s, NEG_INF)
            m_new = jnp.maximum(m, jnp.max(s, axis=0, keepdims=True))
            alpha = jnp.exp(m - m_new)
            p = jnp.exp(s - m_new)
            l = alpha * l + jnp.sum(p, axis=0, keepdims=True)
            acc_ref[...] = alpha * acc_ref[...] + _dot_tn(v, p.astype(BF16))
            return m_new, l

        def sel_mask(kt, kpos):
            expand = jnp.where(
                lax.broadcasted_iota(jnp.int32, (tk, n_sel), 1)
                == kt * blocks_per_tile + (lax.broadcasted_iota(jnp.int32, (tk, n_sel), 0) >> 6),
                1.0, 0.0).astype(BF16)
            chosen = _dot(expand, sel)
            chosen = jnp.concatenate([chosen] * grp, axis=1)
            return (chosen > 0.5) & (kpos <= tpos)

        def win_mask(kt, kpos):
            dist = tpos - kpos
            return (dist >= 0) & (dist < WINDOW)

        init = (jnp.full((1, gt), NEG_INF, F32), jnp.zeros((1, gt), F32))
        acc_ref[...] = jnp.zeros_like(acc_ref)
        _, l_sel = lax.fori_loop(
            0, qi + 1, lambda st, c: attend(ks_ref, vs_ref, qi - st, sel_mask, c), init)
        o_sel = acc_ref[...] / l_sel

        acc_ref[...] = jnp.zeros_like(acc_ref)
        n_win = jnp.minimum(qi, (WINDOW + tk - 1) // tk) + 1
        _, l_win = lax.fori_loop(
            0, n_win, lambda st, c: attend(kw_ref, vw_ref, qi - st, win_mask, c), init)
        o_win = acc_ref[...] / l_win

        def gate_row(branch):
            base = h2 * 16 + branch * grp
            return jnp.concatenate([gates_t[base + g:base + g + 1, :] for g in range(grp)], axis=1)

        o_t = gate_row(0) * o_cmp + gate_row(1) * o_sel + gate_row(2) * o_win
        for g in range(grp):
            col = (h2 * grp + g) * dh
            o_ref[0, :, col:col + dh] = o_t[:, g * tq:(g + 1) * tq].T.astype(o_ref.dtype)


def _nsa_attention(q, q_rot, gates, kvc, kv, *, tq, tk):
    b, s, d = q.shape
    pairs = NSA_KV_HEADS // 2
    pw = 2 * NSA_GROUP * NSA_DH
    n_cmp_pad = kvc.shape[2]
    wt = jnp.asarray(_cmp_to_sel_weights_t(n_cmp_pad, s // SEL_BLOCK), BF16)
    kv_spec = lambda c6: pl.BlockSpec((1, s, LANE), lambda bi, p, i: (bi, 0, 2 * c6 + p))
    return pl.pallas_call(
        functools.partial(_nsa_kernel, tq=tq, tk=tk),
        out_shape=jax.ShapeDtypeStruct((b, s, d), BF16),
        grid=(b, pairs, s // tq),
        in_specs=[
            pl.BlockSpec((1, tq, pw), lambda bi, p, i: (bi, i, p)),
            pl.BlockSpec((1, tq, pw), lambda bi, p, i: (bi, i, p)),
            pl.BlockSpec((1, tq, LANE), lambda bi, p, i: (bi, i, p)),
            pl.BlockSpec((1, 2, n_cmp_pad, LANE), lambda bi, p, i: (bi, p, 0, 0)),
            kv_spec(2), kv_spec(3), kv_spec(4), kv_spec(5),
            pl.BlockSpec(wt.shape, lambda bi, p, i: (0, 0)),
        ],
        out_specs=pl.BlockSpec((1, tq, pw), lambda bi, p, i: (bi, i, p)),
        scratch_shapes=[pltpu.VMEM((NSA_DH, NSA_GROUP * tq), F32)],
        compiler_params=_params("parallel", "parallel", "arbitrary"),
    )(q, q_rot, gates, kvc, kv, kv, kv, kv, wt)


def _rope_tables(seq, width):
    half = NSA_DH // 2
    inv = ROPE_THETA ** (-jnp.arange(half, dtype=F32) / half)
    ang = jnp.arange(seq, dtype=F32)[:, None] * inv[None, :]
    cos, sin = jnp.cos(ang), jnp.sin(ang)
    reps = width // NSA_DH
    cos_full = jnp.tile(jnp.concatenate([cos, cos], axis=1), (1, reps))
    sin_signed = jnp.tile(jnp.concatenate([-sin, sin], axis=1), (1, reps))
    return cos_full, sin_signed


def kernel(x, norm_mix, norm_ffn, gla_w_in, gla_w_alpha_up, gla_b_alpha, gla_norm, gla_w_o, kv_norm, nsa_w_kv, cmp_pe_k, cmp_pe_v, cmp_k_w1, cmp_k_w2, cmp_v_w1, cmp_v_w2, nsa_w_in, nsa_w_o, ffn_w_up, ffn_conv_w, ffn_conv_b, ffn_w_down, norm_final):
    b, s, d = x.shape
    n = b * s
    tm = 1024
    x2 = x.reshape(n, d)

    n_main = 2 * GLA_HEADS * GLA_DK + 2 * GLA_HEADS * GLA_DV
    w_in = gla_w_in[0]
    proj = _norm_proj(x2, norm_mix[0], w_in[:, :n_main].astype(BF16), out_dtype=BF16, tm=tm, tn=256, seq=s)
    w_low = jnp.pad(w_in[:, n_main:], ((0, 0), (0, LANE - GLA_GATE_RANK))).astype(BF16)
    w_up = jnp.pad(gla_w_alpha_up[0], ((0, LANE - GLA_GATE_RANK), (0, 0)))
    w_up3 = jnp.stack(_split3(w_up))
    log_a = _gla_gate(x2, norm_mix[0], w_low, w_up3, gla_b_alpha[0], tm=tm)
    o = _gla_core(proj.reshape(b, s, n_main), log_a.reshape(b, s, -1), gla_norm[0])
    x2 = _proj_res(o.reshape(n, -1), gla_w_o[0].astype(BF16), x2, tm=tm)
    x2 = _conv_ffn(x2.reshape(b, s, d), norm_ffn[0], ffn_w_up[0].astype(BF16), ffn_conv_w[0], ffn_conv_b[0],
                   ffn_w_down[0].astype(BF16), tm=tm, tf=256).reshape(n, d)

    rope = _rope_tables(s, 256)
    kv = _norm_proj(x2, kv_norm, nsa_w_kv.astype(BF16), out_dtype=BF16, tm=tm, tn=256, seq=s,
                    rope=rope, rope_tiles=(2, 4))
    hk, dh = NSA_KV_HEADS, NSA_DH
    kv4 = kv.reshape(b, s, 6, hk, dh)
    rows_per_seq = s // CMP_STRIDE

    def token_groups(t):
        return t.transpose(0, 2, 1, 3).reshape(b * hk * rows_per_seq, CMP_STRIDE * dh)

    kvc = _compress(token_groups(kv4[:, :, 0]), token_groups(kv4[:, :, 1]), cmp_pe_k, cmp_pe_v,
                    cmp_k_w1.astype(BF16), cmp_v_w1.astype(BF16), cmp_k_w2.astype(BF16),
                    cmp_v_w2.astype(BF16), rows_per_seq=rows_per_seq, tr=512)
    kvc = kvc.reshape(b, hk, rows_per_seq, 2 * dh)

    w_nsa = nsa_w_in[0]
    nq = NSA_HEADS * dh
    q, q_rot = _norm_proj(x2, norm_mix[1], w_nsa[:, :nq].astype(BF16), out_dtype=BF16, tm=tm, tn=256, seq=s,
                          rope=rope, rope_second=True, scale=dh ** -0.5)
    wg = w_nsa[:, nq:].reshape(d, 3, hk // 2, 2, NSA_GROUP).transpose(0, 2, 3, 1, 4)
    wg = jnp.pad(wg.reshape(d, hk // 2, 2, 3 * NSA_GROUP), ((0, 0), (0, 0), (0, 0), (0, 16 - 3 * NSA_GROUP)))
    wg = jnp.pad(wg.reshape(d, hk // 2, 32), ((0, 0), (0, 0), (0, LANE - 32))).reshape(d, (hk // 2) * LANE)
    gates = _norm_proj(x2, norm_mix[1], wg.astype(BF16), out_dtype=F32, tm=tm, tn=LANE, seq=s, act="sigmoid")
    o = _nsa_attention(q.reshape(b, s, nq), q_rot.reshape(b, s, nq), gates.reshape(b, s, -1), kvc,
                       kv.reshape(b, s, -1), tq=256, tk=256)
    x2 = _proj_res(o.reshape(n, nq), nsa_w_o[0].astype(BF16), x2, tm=tm)
    out = _conv_ffn(x2.reshape(b, s, d), norm_ffn[1], ffn_w_up[1].astype(BF16), ffn_conv_w[1], ffn_conv_b[1],
                    ffn_w_down[1].astype(BF16), norm_final, tm=tm, tf=256)
    return out
```

```python
import functools
import math

import numpy as np
import jax
import jax.numpy as jnp
from jax import lax
from jax.experimental import pallas as pl
from jax.experimental.pallas import tpu as pltpu

F32 = jnp.float32
BF16 = jnp.bfloat16

EPS = 1e-6
NEG_INF = -1e30
ROPE_THETA = 10000.0
LOG2E = math.log2(math.e)

LANE = 128
BF16_SUBLANES = 16
VMEM_LIMIT_BYTES = 56 * 1024 * 1024

GLA_HEADS = 4
GLA_DK = 128
GLA_DV = 256
GLA_GATE_RANK = 16
GLA_TAU = 16.0
GLA_CHUNK = 64
NSA_HEADS = 16
NSA_KV_HEADS = 4
NSA_GROUP = 4
NSA_DH = 64
CMP_BLOCK = 32
CMP_STRIDE = 16
CMP_HIDDEN = 256
SEL_BLOCK = 64
SEL_TOPK = 16
WINDOW = 512
FORCE_SCORE = 1e4
CONV_WIDTH = 3
CONV_HALO = 16

PROJ_ROWS = 512
PROJ_COLS = 512
RES_ROWS = 1024
GLA_ROWS = 1024
FFN_ROWS = 1024
FFN_COLS = 256
CMP_ROWS = 512
NSA_TQ = 256
NSA_TK = 256
NSA_ACC_ROWS = NSA_DH + BF16_SUBLANES


def _dot(a, b):
    return jnp.dot(a, b, preferred_element_type=F32)


def _dot_nt(a, b):
    return lax.dot_general(a, b, (((1,), (1,)), ((), ())), preferred_element_type=F32)


def _dot_tn(a, b):
    return lax.dot_general(a, b, (((0,), (0,)), ((), ())), preferred_element_type=F32)


def _rms(x, g):
    return x * lax.rsqrt(jnp.mean(x * x, axis=-1, keepdims=True) + EPS) * g


def _split3(x):
    hi = x.astype(BF16)
    r1 = x - hi.astype(F32)
    mid = r1.astype(BF16)
    lo = (r1 - mid.astype(F32)).astype(BF16)
    return hi, mid, lo


def _rope_tile(x, cos, sin_signed):
    w = x.shape[-1]
    lane = lax.broadcasted_iota(jnp.int32, x.shape, 1)
    first_half = (lane & (NSA_DH - 1)) < (NSA_DH // 2)
    partner = jnp.where(first_half, pltpu.roll(x, w - NSA_DH // 2, 1), pltpu.roll(x, NSA_DH // 2, 1))
    return x * cos + partner * sin_signed


def _params(*sem):
    return pltpu.CompilerParams(dimension_semantics=sem, vmem_limit_bytes=VMEM_LIMIT_BYTES)


def _const_spec(shape):
    nd = len(shape)
    return pl.BlockSpec(shape, lambda *_: (0,) * nd, pipeline_mode=pl.Buffered(1))


def _gla_front_kernel(x_ref, g_ref, w_ref, wl_ref, wu_ref, b_ref, o_ref, la_ref):
    hn = _rms(x_ref[...], g_ref[...]).astype(BF16)
    m = w_ref.shape[1]
    for c in range(0, m, PROJ_COLS):
        o_ref[:, c:c + PROJ_COLS] = _dot(hn, w_ref[:, c:c + PROJ_COLS]).astype(o_ref.dtype)
    a_low = _dot(hn, wl_ref[...])
    pieces = _split3(a_low)
    z = b_ref[...]
    for ia, iw in ((0, 0), (0, 1), (1, 0), (0, 2), (1, 1), (2, 0)):
        z = z + _dot(pieces[ia], wu_ref[iw])
    log_sig = jnp.minimum(z, 0.0) - jnp.log(1.0 + jnp.exp(-jnp.abs(z)))
    la_ref[...] = log_sig * (1.0 / GLA_TAU)


def _gla_front(x2, g, w_main, w_low, w_up3, b):
    n, d = x2.shape
    m = w_main.shape[1]
    hk = w_up3.shape[-1]
    tm = PROJ_ROWS
    return pl.pallas_call(
        _gla_front_kernel,
        out_shape=[jax.ShapeDtypeStruct((n, m), BF16), jax.ShapeDtypeStruct((n, hk), F32)],
        grid=(n // tm,),
        in_specs=[
            pl.BlockSpec((tm, d), lambda i: (i, 0)),
            _const_spec((1, d)),
            _const_spec((d, m)),
            _const_spec((d, LANE)),
            _const_spec((3, LANE, hk)),
            _const_spec((1, hk)),
        ],
        out_specs=[pl.BlockSpec((tm, m), lambda i: (i, 0)), pl.BlockSpec((tm, hk), lambda i: (i, 0))],
        compiler_params=_params("parallel"),
    )(x2, g.reshape(1, d), w_main, w_low, w_up3, b.reshape(1, hk))


def _gla_constants():
    c = GLA_CHUNK
    idx = np.arange(c)
    row, col = idx[:, None], idx[None, :]
    mats = [col <= row, col > row]
    masks = []
    h = 1
    while h < c:
        blk = idx // (2 * h)
        upper = (idx % (2 * h)) >= h
        r = (blk * 2 * h + h - 1)[:, None]
        mats.append(np.where(upper[:, None], (col > r) & (col <= row), (col > row) & (col <= r)))
        masks.append((blk[:, None] == blk[None, :]) & upper[:, None] & (~upper[None, :]))
        h *= 2
    masks.append(np.eye(c, dtype=bool))
    return (np.concatenate(mats, 0).astype(np.float32), np.stack(masks).astype(np.float32))


def _gla_kernel(q_ref, k_ref, v_ref, r_ref, g_ref, gs_ref, mk_ref, gn_ref, o_ref, st_ref, *, n_chunks):
    c, dk, dv = GLA_CHUNK, GLA_DK, GLA_DV
    n_levels = mk_ref.shape[0] - 1
    q_scale = dk ** -0.5

    @pl.when(pl.program_id(1) == 0)
    def _():
        st_ref[...] = jnp.zeros_like(st_ref)

    def one_head(h, r0):
        ks, vs = slice(h * dk, (h + 1) * dk), slice(h * dv, (h + 1) * dv)
        q = q_ref[0, pl.ds(r0, c), ks].astype(F32) * q_scale
        k = k_ref[0, pl.ds(r0, c), ks].astype(F32)
        v = v_ref[0, pl.ds(r0, c), vs]
        hi, mid, lo = _split3(g_ref[0, pl.ds(r0, c), ks])
        e3 = _dot(gs_ref[...], jnp.concatenate([hi, mid, lo], axis=1))
        x = jnp.exp(e3[:, :dk] + e3[:, dk:2 * dk] + e3[:, 2 * dk:])
        att = _dot_nt(q.astype(BF16), k.astype(BF16)) * mk_ref[n_levels]
        for lvl in range(n_levels):
            xl = x[(2 + lvl) * c:(3 + lvl) * c]
            att = att + _dot_nt((q * xl).astype(BF16), (k * xl).astype(BF16)) * mk_ref[lvl]
        o = _dot(att.astype(BF16), v)
        st = st_ref[h]
        o = o + _dot_nt((q * x[0:c]).astype(BF16), st.astype(BF16))
        st_ref[h] = st * x[c - 1:c] + _dot_tn(v, (k * x[c:2 * c]).astype(BF16))
        o = o * lax.rsqrt(jnp.mean(o * o, axis=-1, keepdims=True) + EPS) * gn_ref[h:h + 1, :]
        r = r_ref[0, pl.ds(r0, c), vs].astype(F32)
        o_ref[0, pl.ds(r0, c), vs] = (o * (r * jax.nn.sigmoid(r))).astype(o_ref.dtype)

    def body(ci, carry):
        r0 = pl.multiple_of(ci * c, c)
        for h in range(GLA_HEADS):
            one_head(h, r0)
        return carry

    lax.fori_loop(0, n_chunks, body, 0)


def _gla_core(proj, log_a, gla_norm):
    b, s, _ = proj.shape
    h, dk, dv = GLA_HEADS, GLA_DK, GLA_DV
    ts = GLA_ROWS
    gs, mk = _gla_constants()
    return pl.pallas_call(
        functools.partial(_gla_kernel, n_chunks=ts // GLA_CHUNK),
        out_shape=jax.ShapeDtypeStruct((b, s, h * dv), BF16),
        grid=(b, s // ts),
        in_specs=[
            pl.BlockSpec((1, ts, h * dk), lambda i, j: (i, j, 0)),
            pl.BlockSpec((1, ts, h * dk), lambda i, j: (i, j, 1)),
            pl.BlockSpec((1, ts, h * dv), lambda i, j: (i, j, 1)),
            pl.BlockSpec((1, ts, h * dv), lambda i, j: (i, j, 2)),
            pl.BlockSpec((1, ts, h * dk), lambda i, j: (i, j, 0)),
            _const_spec(gs.shape),
            _const_spec(mk.shape),
            _const_spec((h, dv)),
        ],
        out_specs=pl.BlockSpec((1, ts, h * dv), lambda i, j: (i, j, 0)),
        scratch_shapes=[pltpu.VMEM((h, dv, dk), F32)],
        compiler_params=_params("parallel", "arbitrary"),
    )(proj, proj, proj, proj, log_a, jnp.asarray(gs, BF16), jnp.asarray(mk), gla_norm)


def _proj_res_kernel(a_ref, w_ref, r_ref, o_ref):
    o_ref[...] = r_ref[...] + _dot(a_ref[...], w_ref[...])


def _proj_res_t_kernel(a_ref, w_ref, r_ref, o_ref):
    o_ref[...] = r_ref[...] + _dot_tn(a_ref[0], w_ref[...])


def _proj_res(a, w, res, *, transposed=False):
    n, d = res.shape
    k = w.shape[0]
    tm = RES_ROWS
    if transposed:
        per_seq = a.shape[2] // tm
        a_spec = pl.BlockSpec((1, k, tm), lambda i: (i // per_seq, 0, i % per_seq))
    else:
        a_spec = pl.BlockSpec((tm, k), lambda i: (i, 0))
    return pl.pallas_call(
        _proj_res_t_kernel if transposed else _proj_res_kernel,
        out_shape=jax.ShapeDtypeStruct((n, d), F32),
        grid=(n // tm,),
        in_specs=[
            a_spec,
            _const_spec((k, d)),
            pl.BlockSpec((tm, d), lambda i: (i, 0)),
        ],
        out_specs=pl.BlockSpec((tm, d), lambda i: (i, 0)),
        compiler_params=_params("parallel"),
    )(a, w, res)


def _ffn_kernel(*refs, tm, final):
    if final:
        xm_ref, xh_ref, gn_ref, wup_ref, cw_ref, cb_ref, wd_ref, gf_ref, o_ref, hn_ref, u_ref = refs
    else:
        xm_ref, xh_ref, gn_ref, wup_ref, cw_ref, cb_ref, wd_ref, o_ref, hn_ref, u_ref = refs
    halo = CONV_HALO
    nf = wd_ref.shape[0]
    g = gn_ref[...]
    hn_ref[0:halo, :] = _rms(xh_ref[0], g).astype(BF16)
    hn_ref[halo:, :] = _rms(xm_ref[0], g).astype(BF16)
    o_ref[0] = xm_ref[0]
    keep = jnp.where(pl.program_id(1) > 0, 1.0, 0.0)

    def activation(fi, slot):
        hn = hn_ref[...]
        outs = []
        for gv in range(2):
            u = _dot(hn, wup_ref[gv, fi])
            u_ref[slot, gv] = u
            u_ref[slot, gv, 0:halo, :] = u[0:halo] * keep
            out = cb_ref[gv, fi]
            for tap in range(CONV_WIDTH):
                start = halo - (CONV_WIDTH - 1) + tap
                out = out + u_ref[slot, gv, start:start + tm, :] * cw_ref[gv, fi, tap:tap + 1, :]
            outs.append(out)
        gate, val = outs
        return (gate * jax.nn.sigmoid(gate) * val).astype(BF16)

    def pair(pi, carry):
        f0 = 2 * pi
        a0 = activation(f0, 0)
        a1 = activation(f0 + 1, 1)
        o_ref[0] += _dot(a0, wd_ref[f0]) + _dot(a1, wd_ref[f0 + 1])
        return carry

    lax.fori_loop(0, nf // 2, pair, 0)
    if nf % 2:
        o_ref[0] += _dot(activation(nf - 1, 0), wd_ref[nf - 1])
    if final:
        o_ref[0] = _rms(o_ref[0], gf_ref[...])


def _conv_ffn(x3, g, w_up, conv_w, conv_b, w_down, final_gain=None):
    b, s, d = x3.shape
    ffn = w_down.shape[0]
    tm, tf = FFN_ROWS, FFN_COLS
    nf = ffn // tf
    halo = CONV_HALO
    final = final_gain is not None
    w_up_t = w_up.reshape(d, 2, nf, tf).transpose(1, 2, 0, 3)
    conv_w_t = conv_w.reshape(CONV_WIDTH, 2, nf, tf).transpose(1, 2, 0, 3)
    conv_b_t = conv_b.reshape(2, nf, 1, tf)
    w_down_t = w_down.reshape(nf, tf, d)
    in_specs = [
        pl.BlockSpec((1, tm, d), lambda bi, i: (bi, i, 0)),
        pl.BlockSpec((1, halo, d), lambda bi, i: (bi, jnp.maximum(i * (tm // halo) - 1, 0), 0)),
        _const_spec((1, d)),
        _const_spec(w_up_t.shape),
        _const_spec(conv_w_t.shape),
        _const_spec(conv_b_t.shape),
        _const_spec(w_down_t.shape),
    ]
    args = [x3, x3, g.reshape(1, d), w_up_t, conv_w_t, conv_b_t, w_down_t]
    if final:
        in_specs.append(_const_spec((1, d)))
        args.append(final_gain.reshape(1, d))
    return pl.pallas_call(
        functools.partial(_ffn_kernel, tm=tm, final=final),
        out_shape=jax.ShapeDtypeStruct((b, s, d), F32),
        grid=(b, s // tm),
        in_specs=in_specs,
        out_specs=pl.BlockSpec((1, tm, d), lambda bi, i: (bi, i, 0)),
        scratch_shapes=[
            pltpu.VMEM((tm + halo, d), BF16),
            pltpu.VMEM((2, 2, tm + halo, tf), F32),
        ],
        compiler_params=_params("parallel", "parallel"),
    )(*args)


def _nsa_front_kernel(x_ref, gkv_ref, gq_ref, wkv_ref, wq_ref, wg_ref, cos_ref, sin_ref,
                      kv_ref, q_ref, qr_ref, gt_ref, *, q_scale):
    x = x_ref[...]
    y = x * lax.rsqrt(jnp.mean(x * x, axis=-1, keepdims=True) + EPS)
    hn_kv = (y * gkv_ref[...]).astype(BF16)
    hn_q = (y * gq_ref[...]).astype(BF16)
    cos, sin = cos_ref[...], sin_ref[...]
    rope_w = cos.shape[1]

    for grp in range(wkv_ref.shape[1] // rope_w):
        acc = _dot(hn_kv, wkv_ref[:, grp * rope_w:(grp + 1) * rope_w])
        if grp in (2, 4):
            acc = _rope_tile(acc, cos, sin)
        kv_ref[:, grp * rope_w:(grp + 1) * rope_w] = acc.astype(kv_ref.dtype)
    for grp in range(wq_ref.shape[1] // rope_w):
        acc = _dot(hn_q, wq_ref[:, grp * rope_w:(grp + 1) * rope_w]) * q_scale
        q_ref[0, grp * rope_w:(grp + 1) * rope_w, :] = acc.T.astype(q_ref.dtype)
        qr_ref[0, grp * rope_w:(grp + 1) * rope_w, :] = _rope_tile(acc, cos, sin).T.astype(qr_ref.dtype)
    gt_ref[0] = jax.nn.sigmoid(_dot(hn_q, wg_ref[...])).T


def _nsa_front(x2, g_kv, g_q, w_kv, w_q, w_g, rope, *, seq, q_scale):
    n, d = x2.shape
    tm = PROJ_ROWS
    per_seq = seq // tm
    batch = n // seq
    rope_w = rope[0].shape[1]
    row = lambda m: pl.BlockSpec((tm, m), lambda i: (i, 0))
    col = lambda m: pl.BlockSpec((1, m, tm), lambda i: (i // per_seq, 0, i % per_seq))
    return pl.pallas_call(
        functools.partial(_nsa_front_kernel, q_scale=q_scale),
        out_shape=[
            jax.ShapeDtypeStruct((n, w_kv.shape[1]), BF16),
            jax.ShapeDtypeStruct((batch, w_q.shape[1], seq), BF16),
            jax.ShapeDtypeStruct((batch, w_q.shape[1], seq), BF16),
            jax.ShapeDtypeStruct((batch, w_g.shape[1], seq), F32),
        ],
        grid=(n // tm,),
        in_specs=[
            row(d), _const_spec((1, d)), _const_spec((1, d)),
            _const_spec(w_kv.shape), _const_spec(w_q.shape), _const_spec(w_g.shape),
            pl.BlockSpec((tm, rope_w), lambda i: (i % per_seq, 0)),
            pl.BlockSpec((tm, rope_w), lambda i: (i % per_seq, 0)),
        ],
        out_specs=[row(w_kv.shape[1]), col(w_q.shape[1]), col(w_q.shape[1]), col(w_g.shape[1])],
        compiler_params=_params("parallel"),
    )(x2, g_kv.reshape(1, d), g_q.reshape(1, d), w_kv, w_q, w_g, *rope)


def _compress_kernel(tk_ref, tv_ref, w1k_ref, w1v_ref, bk_ref, bv_ref, w2k_ref, w2v_ref, o_ref, *, rows_per_seq):
    rows = tk_ref.shape[0]
    row = lax.broadcasted_iota(jnp.int32, (rows, 1), 0)
    valid = jnp.where((row & (rows_per_seq - 1)) < rows_per_seq - 1, 1.0, 0.0)

    def one(t_ref, w1_ref, b_ref, w2_ref):
        y = _dot(t_ref[...], w1_ref[...])
        hid = y.shape[1] // 2
        pre = y[:, :hid] + pltpu.roll(y[:, hid:], rows - 1, 0) + b_ref[0:1, :]
        out = _dot(jax.nn.gelu(pre).astype(BF16), w2_ref[...])
        return out * valid

    o_ref[...] = jnp.concatenate(
        [one(tk_ref, w1k_ref, bk_ref, w2k_ref), one(tv_ref, w1v_ref, bv_ref, w2v_ref)], axis=1
    ).astype(o_ref.dtype)


def _pe_bias_kernel(pk_ref, pv_ref, w1k_ref, w1v_ref, ok_ref, ov_ref):
    ok_ref[...] = _dot(pk_ref[...], w1k_ref[...])
    ov_ref[...] = _dot(pv_ref[...], w1v_ref[...])


def _compress(tk, tv, pe_k, pe_v, w1k, w1v, w2k, w2v, *, rows_per_seq):
    r, kdim = tk.shape
    hid = w1k.shape[1]
    dh = w2k.shape[1]
    tr = CMP_ROWS
    sub = 8
    pk = jnp.broadcast_to(pe_k.reshape(1, -1), (sub, 2 * kdim)).astype(BF16)
    pv = jnp.broadcast_to(pe_v.reshape(1, -1), (sub, 2 * kdim)).astype(BF16)
    bk, bv = pl.pallas_call(
        _pe_bias_kernel,
        out_shape=[jax.ShapeDtypeStruct((sub, hid), F32)] * 2,
    )(pk, pv, w1k, w1v)
    w1k_cat = jnp.concatenate([w1k[:kdim], w1k[kdim:]], axis=1)
    w1v_cat = jnp.concatenate([w1v[:kdim], w1v[kdim:]], axis=1)
    return pl.pallas_call(
        functools.partial(_compress_kernel, rows_per_seq=rows_per_seq),
        out_shape=jax.ShapeDtypeStruct((r, 2 * dh), BF16),
        grid=(r // tr,),
        in_specs=[
            pl.BlockSpec((tr, kdim), lambda i: (i, 0)),
            pl.BlockSpec((tr, kdim), lambda i: (i, 0)),
            _const_spec((kdim, 2 * hid)),
            _const_spec((kdim, 2 * hid)),
            _const_spec((sub, hid)),
            _const_spec((sub, hid)),
            _const_spec((hid, dh)),
            _const_spec((hid, dh)),
        ],
        out_specs=pl.BlockSpec((tr, 2 * dh), lambda i: (i, 0)),
        compiler_params=_params("parallel"),
    )(tk, tv, w1k_cat, w1v_cat, bk, bv, w2k, w2v)


def _cmp_to_sel_weights_t(n_cmp_pad, n_sel):
    c0 = np.arange(n_cmp_pad)[None, :] * CMP_STRIDE
    s0 = np.arange(n_sel)[:, None] * SEL_BLOCK
    ov = np.clip(np.minimum(c0 + CMP_BLOCK, s0 + SEL_BLOCK) - np.maximum(c0, s0), 0, None)
    return (ov / CMP_BLOCK).astype(np.float32)


def _nsa_kernel(q_ref, qr_ref, gt_ref, kc_ref, ks_ref, vs_ref, kw_ref, vw_ref, wt_ref, oh_ref, o_ref,
                ksel_s, kwin_s, vsel_s, vwin_s, vcmp_s, m_s, acc_s, *, tq, tk):
    grp, dh = NSA_GROUP, NSA_DH
    heads = 2 * grp
    qi = pl.program_id(2)
    t0 = qi * tq
    n_sel = wt_ref.shape[0]
    n_cmp_pad = wt_ref.shape[1]
    seq = ks_ref.shape[1]

    @pl.when(qi == 0)
    def _():
        ones_row = jnp.where(lax.broadcasted_iota(jnp.int32, (BF16_SUBLANES, tk), 0) == 0, 1.0, 0.0).astype(BF16)
        for h2 in range(2):
            cols = slice(h2 * dh, (h2 + 1) * dh)
            ksel_s[h2, :, 0:dh] = ks_ref[0, :, cols]
            ksel_s[h2, :, dh:2 * dh] = oh_ref[...]
            kwin_s[h2, :, 0:dh] = kw_ref[0, :, cols]
            kwin_s[h2, :, dh:2 * dh] = jnp.zeros((seq, dh), BF16)
            vcmp_s[h2] = kc_ref[0, h2, :, dh:2 * dh].astype(F32).T.astype(BF16)

        def build(j, carry):
            r0 = pl.multiple_of(j * tk, tk)
            for src, dst in ((vs_ref, vsel_s), (vw_ref, vwin_s)):
                blk_t = src[0, pl.ds(r0, tk), :].astype(F32).T
                for h2 in range(2):
                    dst[h2, j, 0:dh, :] = blk_t[h2 * dh:(h2 + 1) * dh].astype(BF16)
                    dst[h2, j, dh:, :] = ones_row
            return carry

        lax.fori_loop(0, seq // tk, build, 0)

    tpos = t0 + lax.broadcasted_iota(jnp.int32, (1, tq), 1)
    key_iota = lax.broadcasted_iota(jnp.int32, (tk, 1), 0)
    cmp_end = lax.broadcasted_iota(jnp.int32, (n_cmp_pad, 1), 0) * CMP_STRIDE + (CMP_BLOCK - 1)
    cmp_mask = cmp_end <= tpos
    blk = lax.broadcasted_iota(jnp.int32, (n_sel, 1), 0)
    cur = tpos >> 6
    forced = (blk == 0) | (blk == cur) | (blk == cur - 1)
    win_lo = jnp.maximum(tpos - WINDOW, -1)

    def run_tiles(tiles, q_augs):
        chunks = []
        for sub in range(tq // tk):
            for branch, k_s, v_s, h2, tile, mode in tiles:
                kt = jnp.maximum(tile, 0) * (tq // tk) + sub
                k0 = pl.multiple_of(kt * tk, tk)
                shared = dict(k=k_s[h2, pl.ds(k0, tk), :], v=v_s[h2, kt], mode=mode,
                              kpos=tile * tq + sub * tk + key_iota)
                for g in range(grp):
                    chunks.append(dict(shared, slot=(h2 * 2 + branch) * grp + g, q=q_augs[h2][g]))

        def scores(c):
            s = _dot(c["k"], c["q"])
            if c["mode"] == "causal":
                s = jnp.where(c["kpos"] <= tpos, s, NEG_INF)
            elif c["mode"] == "window_start":
                s = jnp.where(c["kpos"] > win_lo, s, NEG_INF)
            elif c["mode"] == "in_sequence":
                s = jnp.where(c["kpos"] >= 0, s, NEG_INF)
            c["s"] = s

        def softmax_update(c):
            m = m_s[c["slot"]]
            m_new = jnp.maximum(m, jnp.max(c["s"], axis=0, keepdims=True))
            c["p"] = jnp.exp2(c.pop("s") - m_new).astype(BF16)
            c["alpha"] = jnp.exp2(m - m_new)
            m_s[c["slot"]] = m_new

        def accumulate(c):
            acc_s[c["slot"]] = c["alpha"] * acc_s[c["slot"]] + _dot(c["v"], c.pop("p"))

        ahead = 2
        for c in chunks[:ahead]:
            scores(c)
        for i, c in enumerate(chunks):
            softmax_update(c)
            if i + ahead < len(chunks):
                scores(chunks[i + ahead])
            accumulate(c)

    def result(branch, h2, g):
        slot = (h2 * 2 + branch) * grp + g
        return acc_s[slot, 0:dh, :] / acc_s[slot, dh:dh + 1, :]

    m_s[...] = jnp.full(m_s.shape, NEG_INF, F32)
    acc_s[...] = jnp.zeros(acc_s.shape, F32)
    q_augs, o_cmps = [], []

    for h2 in range(2):
        head_rows = lambda ref, g: ref[0, (h2 * grp + g) * dh:(h2 * grp + g + 1) * dh, :]

        k_cmp = kc_ref[0, h2, :, 0:dh]
        p_grp = jnp.zeros((n_cmp_pad, tq), F32)
        o_cmp = []
        for g in range(grp):
            s = jnp.where(cmp_mask, _dot(k_cmp, head_rows(q_ref, g)), NEG_INF)
            e = jnp.where(cmp_mask, jnp.exp2(s - jnp.max(s, axis=0, keepdims=True)), 0.0)
            inv = 1.0 / jnp.maximum(jnp.sum(e, axis=0, keepdims=True), 1e-30)
            o_cmp.append(_dot(vcmp_s[h2], e.astype(BF16)) * inv)
            p_grp = p_grp + e * inv
        i3 = _dot(wt_ref[...], jnp.concatenate(_split3(p_grp), axis=1))
        imp = i3[:, 0:tq] + i3[:, tq:2 * tq] + i3[:, 2 * tq:]
        imp = jnp.where(blk > cur, -1.0, jnp.where(forced, FORCE_SCORE, imp))
        rank = jnp.zeros((n_sel, tq), F32)
        for jp in range(n_sel):
            other = imp[jp:jp + 1, :]
            ge = jnp.where(other >= imp, 1.0, 0.0)
            gt_ = jnp.where(other > imp, 1.0, 0.0)
            rank = rank + jnp.where(blk > jp, ge, gt_)
        sel_bias = jnp.where(rank < float(SEL_TOPK), 0.0, NEG_INF).astype(BF16)
        pad = jnp.zeros((dh - n_sel, tq), BF16)
        q_augs.append([jnp.concatenate([head_rows(qr_ref, g), sel_bias, pad], axis=0) for g in range(grp)])
        o_cmps.append(o_cmp)

    sel, win = 0, 1

    def sel_body(kt, carry):
        run_tiles([(sel, ksel_s, vsel_s, h2, kt, None) for h2 in range(2)], q_augs)
        return carry

    lax.fori_loop(0, qi, sel_body, 0)
    run_tiles([(sel, ksel_s, vsel_s, h2, qi, "causal") for h2 in range(2)]
              + [(win, kwin_s, vwin_s, h2, qi - 2, "window_start") for h2 in range(2)]
              + [(win, kwin_s, vwin_s, h2, qi - 1, "in_sequence") for h2 in range(2)]
              + [(win, kwin_s, vwin_s, h2, qi, "causal") for h2 in range(2)], q_augs)

    for h2 in range(2):
        for g in range(grp):
            hg = h2 * grp + g
            gate = lambda branch: gt_ref[0, branch * heads + hg:branch * heads + hg + 1, :]
            o_ref[0, hg * dh:(hg + 1) * dh, :] = (
                gate(0) * o_cmps[h2][g] + gate(1) * result(sel, h2, g) + gate(2) * result(win, h2, g)
            ).astype(o_ref.dtype)


def _nsa_attention(q_t, qr_t, gates_t, kvc, kv):
    b, d, s = q_t.shape
    tq, tk = NSA_TQ, NSA_TK
    assert WINDOW == 2 * tq and tq % SEL_BLOCK == 0 and tq % tk == 0
    dh = NSA_DH
    pairs = NSA_KV_HEADS // 2
    heads = 2 * NSA_GROUP
    pw = heads * dh
    n_cmp_pad = kvc.shape[2]
    n_sel = s // SEL_BLOCK
    wt = jnp.asarray(_cmp_to_sel_weights_t(n_cmp_pad, n_sel), BF16)
    onehot = np.zeros((s, dh), np.float32)
    onehot[np.arange(s), np.arange(s) // SEL_BLOCK] = 1.0
    kv_spec = lambda c6: pl.BlockSpec((1, s, LANE), lambda bi, p, i: (bi, 0, 2 * c6 + p))
    return pl.pallas_call(
        functools.partial(_nsa_kernel, tq=tq, tk=tk),
        out_shape=jax.ShapeDtypeStruct((b, d, s), BF16),
        grid=(b, pairs, s // tq),
        in_specs=[
            pl.BlockSpec((1, pw, tq), lambda bi, p, i: (bi, p, i)),
            pl.BlockSpec((1, pw, tq), lambda bi, p, i: (bi, p, i)),
            pl.BlockSpec((1, LANE, tq), lambda bi, p, i: (bi, p, i)),
            pl.BlockSpec((1, 2, n_cmp_pad, LANE), lambda bi, p, i: (bi, p, 0, 0)),
            kv_spec(2), kv_spec(3), kv_spec(4), kv_spec(5),
            _const_spec(wt.shape),
            _const_spec(onehot.shape),
        ],
        out_specs=pl.BlockSpec((1, pw, tq), lambda bi, p, i: (bi, p, i)),
        scratch_shapes=[
            pltpu.VMEM((2, s, 2 * dh), BF16),
            pltpu.VMEM((2, s, 2 * dh), BF16),
            pltpu.VMEM((2, s // tk, NSA_ACC_ROWS, tk), BF16),
            pltpu.VMEM((2, s // tk, NSA_ACC_ROWS, tk), BF16),
            pltpu.VMEM((2, dh, n_cmp_pad), BF16),
            pltpu.VMEM((2 * heads, 1, tq), F32),
            pltpu.VMEM((2 * heads, NSA_ACC_ROWS, tq), F32),
        ],
        compiler_params=_params("parallel", "parallel", "arbitrary"),
    )(q_t, qr_t, gates_t, kvc, kv, kv, kv, kv, wt, jnp.asarray(onehot, BF16))


def _rope_tables(seq, width):
    half = NSA_DH // 2
    inv = ROPE_THETA ** (-jnp.arange(half, dtype=F32) / half)
    ang = jnp.arange(seq, dtype=F32)[:, None] * inv[None, :]
    cos, sin = jnp.cos(ang), jnp.sin(ang)
    reps = width // NSA_DH
    cos_full = jnp.tile(jnp.concatenate([cos, cos], axis=1), (1, reps))
    sin_signed = jnp.tile(jnp.concatenate([-sin, sin], axis=1), (1, reps))
    return cos_full, sin_signed


def _nsa_gate_weights(w_gate):
    d = w_gate.shape[0]
    pairs = NSA_KV_HEADS // 2
    heads = 2 * NSA_GROUP
    wg = w_gate.reshape(d, 3, pairs, heads).transpose(0, 2, 1, 3).reshape(d, pairs, 3 * heads)
    return jnp.pad(wg, ((0, 0), (0, 0), (0, LANE - 3 * heads))).reshape(d, pairs * LANE)


def kernel(x, norm_mix, norm_ffn, gla_w_in, gla_w_alpha_up, gla_b_alpha, gla_norm, gla_w_o, kv_norm, nsa_w_kv, cmp_pe_k, cmp_pe_v, cmp_k_w1, cmp_k_w2, cmp_v_w1, cmp_v_w2, nsa_w_in, nsa_w_o, ffn_w_up, ffn_conv_w, ffn_conv_b, ffn_w_down, norm_final):
    b, s, d = x.shape
    n = b * s
    x2 = x.reshape(n, d)

    n_main = 2 * GLA_HEADS * GLA_DK + 2 * GLA_HEADS * GLA_DV
    w_in = gla_w_in[0]
    w_low = jnp.pad(w_in[:, n_main:], ((0, 0), (0, LANE - GLA_GATE_RANK))).astype(BF16)
    w_up = jnp.pad(gla_w_alpha_up[0], ((0, LANE - GLA_GATE_RANK), (0, 0)))
    proj, log_a = _gla_front(x2, norm_mix[0], w_in[:, :n_main].astype(BF16), w_low, jnp.stack(_split3(w_up)),
                             gla_b_alpha[0])
    o = _gla_core(proj.reshape(b, s, n_main), log_a.reshape(b, s, -1), gla_norm[0])
    x2 = _proj_res(o.reshape(n, -1), gla_w_o[0].astype(BF16), x2)
    x2 = _conv_ffn(x2.reshape(b, s, d), norm_ffn[0], ffn_w_up[0].astype(BF16), ffn_conv_w[0], ffn_conv_b[0],
                   ffn_w_down[0].astype(BF16)).reshape(n, d)

    hk, dh = NSA_KV_HEADS, NSA_DH
    nq = NSA_HEADS * dh
    w_nsa = nsa_w_in[0]
    kv, q_t, qr_t, gates_t = _nsa_front(
        x2, kv_norm, norm_mix[1], nsa_w_kv.astype(BF16), w_nsa[:, :nq].astype(BF16),
        _nsa_gate_weights(w_nsa[:, nq:]).astype(BF16), _rope_tables(s, 4 * dh), seq=s, q_scale=dh ** -0.5 * LOG2E)
    kv4 = kv.reshape(b, s, 6, hk, dh)
    rows_per_seq = s // CMP_STRIDE

    def token_groups(t):
        return t.transpose(0, 2, 1, 3).reshape(b * hk * rows_per_seq, CMP_STRIDE * dh)

    kvc = _compress(token_groups(kv4[:, :, 0]), token_groups(kv4[:, :, 1]), cmp_pe_k, cmp_pe_v,
                    cmp_k_w1.astype(BF16), cmp_v_w1.astype(BF16), cmp_k_w2.astype(BF16),
                    cmp_v_w2.astype(BF16), rows_per_seq=rows_per_seq)
    kvc = kvc.reshape(b, hk, rows_per_seq, 2 * dh)

    o_t = _nsa_attention(q_t, qr_t, gates_t, kvc, kv.reshape(b, s, -1))
    x2 = _proj_res(o_t, nsa_w_o[0].astype(BF16), x2, transposed=True)
    return _conv_ffn(x2.reshape(b, s, d), norm_ffn[1], ffn_w_up[1].astype(BF16), ffn_conv_w[1], ffn_conv_b[1],
                     ffn_w_down[1].astype(BF16), norm_final)
```

```python
import functools
import math

import numpy as np
import jax
import jax.numpy as jnp
from jax import lax
from jax.experimental import pallas as pl
from jax.experimental.pallas import tpu as pltpu

F32 = jnp.float32
BF16 = jnp.bfloat16

EPS = 1e-6
NEG_INF = -1e30
ROPE_THETA = 10000.0
LOG2E = math.log2(math.e)

LANE = 128
BF16_SUBLANES = 16
VMEM_LIMIT_BYTES = 56 * 1024 * 1024

GLA_HEADS = 4
GLA_DK = 128
GLA_DV = 256
GLA_GATE_RANK = 16
GLA_TAU = 16.0
GLA_CHUNK = 64
GLA_LEVEL_GROUP = 4
GATE_PIECE_PAIRS = ((0, 0), (0, 1), (1, 0), (0, 2), (1, 1), (2, 0))
NSA_HEADS = 16
NSA_KV_HEADS = 4
NSA_GROUP = 4
NSA_DH = 64
CMP_BLOCK = 32
CMP_STRIDE = 16
CMP_HIDDEN = 256
SEL_BLOCK = 64
SEL_TOPK = 16
WINDOW = 512
FORCE_SCORE = 1e4
CONV_WIDTH = 3
CONV_HALO = 16

PROJ_ROWS = 1024
PROJ_COLS = 512
RES_ROWS = 1024
GLA_ROWS = 1024
FFN_ROWS = 1024
FFN_COLS = 256
CMP_ROWS = 512
NSA_TQ = 256
NSA_TK = 256
NSA_ACC_ROWS = NSA_DH + BF16_SUBLANES


def _dot(a, b):
    return jnp.dot(a, b, preferred_element_type=F32)


def _dot_nt(a, b):
    return lax.dot_general(a, b, (((1,), (1,)), ((), ())), preferred_element_type=F32)


def _dot_tn(a, b):
    return lax.dot_general(a, b, (((0,), (0,)), ((), ())), preferred_element_type=F32)


def _rms(x, g):
    return x * lax.rsqrt(jnp.mean(x * x, axis=-1, keepdims=True) + EPS) * g


def _split3(x):
    hi = x.astype(BF16)
    r1 = x - hi.astype(F32)
    mid = r1.astype(BF16)
    lo = (r1 - mid.astype(F32)).astype(BF16)
    return hi, mid, lo


def _rope_tile(x, cos, sin_signed):
    w = x.shape[-1]
    lane = lax.broadcasted_iota(jnp.int32, x.shape, 1)
    first_half = (lane & (NSA_DH - 1)) < (NSA_DH // 2)
    partner = jnp.where(first_half, pltpu.roll(x, w - NSA_DH // 2, 1), pltpu.roll(x, NSA_DH // 2, 1))
    return x * cos + partner * sin_signed


def _params(*sem):
    return pltpu.CompilerParams(dimension_semantics=sem, vmem_limit_bytes=VMEM_LIMIT_BYTES)


def _const_spec(shape):
    nd = len(shape)
    return pl.BlockSpec(shape, lambda *_: (0,) * nd, pipeline_mode=pl.Buffered(1))


def _gla_front_kernel(x_ref, g_ref, w_ref, wl_ref, wu_ref, b_ref, o_ref, la_ref):
    hn = _rms(x_ref[...], g_ref[...]).astype(BF16)
    m = w_ref.shape[1]
    for c in range(0, m, PROJ_COLS):
        o_ref[:, c:c + PROJ_COLS] = _dot(hn, w_ref[:, c:c + PROJ_COLS]).astype(o_ref.dtype)
    a_rep = _dot(hn, wl_ref[...])
    hi, mid, lo = _split3(a_rep)
    group = lax.broadcasted_iota(jnp.int32, a_rep.shape, 1) // GLA_GATE_RANK
    a_piece = [g for g, (ia, _) in enumerate(GATE_PIECE_PAIRS)]
    is_piece = lambda p: functools.reduce(
        jnp.logical_or, [group == g for g in a_piece if GATE_PIECE_PAIRS[g][0] == p])
    a6 = jnp.where(is_piece(2), lo, jnp.where(is_piece(1), mid, hi))
    z = b_ref[...] + _dot(a6, wu_ref[...])
    log_sig = jnp.minimum(z, 0.0) - jnp.log(1.0 + jnp.exp(-jnp.abs(z)))
    la_ref[...] = log_sig * (1.0 / GLA_TAU)


def _gla_front(x2, g, w_main, w_low, w_up6, b):
    n, d = x2.shape
    m = w_main.shape[1]
    hk = w_up6.shape[-1]
    tm = PROJ_ROWS
    return pl.pallas_call(
        _gla_front_kernel,
        out_shape=[jax.ShapeDtypeStruct((n, m), BF16), jax.ShapeDtypeStruct((n, hk), F32)],
        grid=(n // tm,),
        in_specs=[
            pl.BlockSpec((tm, d), lambda i: (i, 0)),
            _const_spec((1, d)),
            _const_spec((d, m)),
            _const_spec((d, LANE)),
            _const_spec((LANE, hk)),
            _const_spec((1, hk)),
        ],
        out_specs=[pl.BlockSpec((tm, m), lambda i: (i, 0)), pl.BlockSpec((tm, hk), lambda i: (i, 0))],
        compiler_params=_params("parallel"),
    )(x2, g.reshape(1, d), w_main, w_low, w_up6, b.reshape(1, hk))


def _gla_constants():
    c = GLA_CHUNK
    idx = np.arange(c)
    row, col = idx[:, None], idx[None, :]
    mats = [col <= row, col > row]
    masks = []
    h = 1
    while h < c:
        blk = idx // (2 * h)
        upper = (idx % (2 * h)) >= h
        r = (blk * 2 * h + h - 1)[:, None]
        mats.append(np.where(upper[:, None], (col > r) & (col <= row), (col > row) & (col <= r)))
        masks.append((blk[:, None] == blk[None, :]) & upper[:, None] & (~upper[None, :]))
        h *= 2
    masks.append(np.eye(c, dtype=bool))
    strips = np.zeros((len(masks), c, LANE), np.float32)
    for lvl, m in enumerate(masks):
        half = (lvl % GLA_LEVEL_GROUP) % (LANE // c)
        strips[lvl, :, half * c:(half + 1) * c] = m
    return np.concatenate(mats, 0).astype(np.float32), strips


def _gla_kernel(q_ref, k_ref, v_ref, r_ref, g_ref, gs_ref, mk_ref, gn_ref, o_ref, st_ref, *, n_chunks):
    c, dk, dv = GLA_CHUNK, GLA_DK, GLA_DV
    n_levels = mk_ref.shape[0]
    per_strip = LANE // c
    q_scale = dk ** -0.5

    @pl.when(pl.program_id(1) == 0)
    def _():
        st_ref[...] = jnp.zeros_like(st_ref)

    def decays(h, r0, t):
        ks = slice(h * dk, (h + 1) * dk)
        t["q"] = q_ref[0, pl.ds(r0, c), ks].astype(F32) * q_scale
        t["k"] = k_ref[0, pl.ds(r0, c), ks].astype(F32)
        t["v"] = v_ref[0, pl.ds(r0, c), h * dv:(h + 1) * dv]
        hi, mid, _ = _split3(g_ref[0, pl.ds(r0, c), ks])
        t["e2"] = _dot(gs_ref[...], jnp.concatenate([hi, mid], axis=1))

    def level_products(t):
        e2 = t.pop("e2")
        x = t["x"] = jnp.exp(e2[:, :dk] + e2[:, dk:])
        q, k = t["q"], t["k"]
        t["prods"] = []
        for first in range(0, n_levels, GLA_LEVEL_GROUP):
            lvls = list(range(first, min(first + GLA_LEVEL_GROUP, n_levels)))
            factor = lambda lvl: x[(2 + lvl) * c:(3 + lvl) * c] if lvl < n_levels - 1 else 1.0
            q_stack = jnp.concatenate([(q * factor(lvl)).astype(BF16) for lvl in lvls], axis=0)
            k_stack = jnp.concatenate([(k * factor(lvl)).astype(BF16) for lvl in lvls]
                                      + [jnp.zeros((c, dk), BF16)] * (GLA_LEVEL_GROUP - len(lvls)), axis=0)
            t["prods"].append((lvls, _dot_nt(q_stack, k_stack)))
        t["q_inter"] = (q * x[0:c]).astype(BF16)
        t["k_state"] = (k * x[c:2 * c]).astype(BF16)

    def outputs(h, t):
        strip = jnp.zeros((c, LANE), F32)
        for lvls, prod in t.pop("prods"):
            for j, lvl in enumerate(lvls):
                col = (j // per_strip) * LANE
                strip = strip + prod[j * c:(j + 1) * c, col:col + LANE] * mk_ref[lvl]
        att = strip[:, 0:c]
        for part in range(1, per_strip):
            att = att + strip[:, part * c:(part + 1) * c]
        st = st_ref[h]
        t["o"] = _dot(att.astype(BF16), t["v"]) + _dot_nt(t["q_inter"], st.astype(BF16))
        st_ref[h] = st * t["x"][c - 1:c] + _dot_tn(t["v"], t["k_state"])

    def finish(h, r0, t):
        o = t["o"]
        o = o * lax.rsqrt(jnp.mean(o * o, axis=-1, keepdims=True) + EPS) * gn_ref[h:h + 1, :]
        r = r_ref[0, pl.ds(r0, c), h * dv:(h + 1) * dv].astype(F32)
        o_ref[0, pl.ds(r0, c), h * dv:(h + 1) * dv] = (o * (r * jax.nn.sigmoid(r))).astype(o_ref.dtype)

    def body(ci, carry):
        r0 = pl.multiple_of(ci * c, c)
        heads = [dict() for _ in range(GLA_HEADS)]
        for h, t in enumerate(heads):
            decays(h, r0, t)
        for t in heads:
            level_products(t)
        for h, t in enumerate(heads):
            outputs(h, t)
        for h, t in enumerate(heads):
            finish(h, r0, t)
        return carry

    lax.fori_loop(0, n_chunks, body, 0)


def _gla_core(proj, log_a, gla_norm):
    b, s, _ = proj.shape
    h, dk, dv = GLA_HEADS, GLA_DK, GLA_DV
    ts = GLA_ROWS
    gs, mk = _gla_constants()
    return pl.pallas_call(
        functools.partial(_gla_kernel, n_chunks=ts // GLA_CHUNK),
        out_shape=jax.ShapeDtypeStruct((b, s, h * dv), BF16),
        grid=(b, s // ts),
        in_specs=[
            pl.BlockSpec((1, ts, h * dk), lambda i, j: (i, j, 0)),
            pl.BlockSpec((1, ts, h * dk), lambda i, j: (i, j, 1)),
            pl.BlockSpec((1, ts, h * dv), lambda i, j: (i, j, 1)),
            pl.BlockSpec((1, ts, h * dv), lambda i, j: (i, j, 2)),
            pl.BlockSpec((1, ts, h * dk), lambda i, j: (i, j, 0)),
            _const_spec(gs.shape),
            _const_spec(mk.shape),
            _const_spec((h, dv)),
        ],
        out_specs=pl.BlockSpec((1, ts, h * dv), lambda i, j: (i, j, 0)),
        scratch_shapes=[pltpu.VMEM((h, dv, dk), F32)],
        compiler_params=_params("parallel", "arbitrary"),
    )(proj, proj, proj, proj, log_a, jnp.asarray(gs, BF16), jnp.asarray(mk), gla_norm)


def _proj_res_kernel(a_ref, w_ref, r_ref, o_ref):
    o_ref[...] = r_ref[...] + _dot(a_ref[...], w_ref[...])


def _proj_res_t_kernel(a_ref, w_ref, r_ref, o_ref):
    o_ref[...] = r_ref[...] + _dot_tn(a_ref[0], w_ref[...])


def _proj_res(a, w, res, *, transposed=False):
    n, d = res.shape
    k = w.shape[0]
    tm = RES_ROWS
    if transposed:
        per_seq = a.shape[2] // tm
        a_spec = pl.BlockSpec((1, k, tm), lambda i: (i // per_seq, 0, i % per_seq))
    else:
        a_spec = pl.BlockSpec((tm, k), lambda i: (i, 0))
    return pl.pallas_call(
        _proj_res_t_kernel if transposed else _proj_res_kernel,
        out_shape=jax.ShapeDtypeStruct((n, d), F32),
        grid=(n // tm,),
        in_specs=[
            a_spec,
            _const_spec((k, d)),
            pl.BlockSpec((tm, d), lambda i: (i, 0)),
        ],
        out_specs=pl.BlockSpec((tm, d), lambda i: (i, 0)),
        compiler_params=_params("parallel"),
    )(a, w, res)


def _ffn_kernel(*refs, tm, final):
    if final:
        xm_ref, xh_ref, gn_ref, wup_ref, cw_ref, cb_ref, wd_ref, gf_ref, o_ref, hn_ref, u_ref = refs
    else:
        xm_ref, xh_ref, gn_ref, wup_ref, cw_ref, cb_ref, wd_ref, o_ref, hn_ref, u_ref = refs
    halo = CONV_HALO
    nf = wd_ref.shape[0]
    g = gn_ref[...]
    hn_ref[0:halo, :] = _rms(xh_ref[0], g).astype(BF16)
    hn_ref[halo:, :] = _rms(xm_ref[0], g).astype(BF16)
    o_ref[0] = xm_ref[0]
    keep = jnp.where(pl.program_id(1) > 0, 1.0, 0.0)

    def activation(fi, slot):
        hn = hn_ref[...]
        outs = []
        for gv in range(2):
            u = _dot(hn, wup_ref[gv, fi])
            u_ref[slot, gv] = u
            u_ref[slot, gv, 0:halo, :] = u[0:halo] * keep
            out = cb_ref[gv, fi]
            for tap in range(CONV_WIDTH):
                start = halo - (CONV_WIDTH - 1) + tap
                out = out + u_ref[slot, gv, start:start + tm, :] * cw_ref[gv, fi, tap:tap + 1, :]
            outs.append(out)
        gate, val = outs
        return (gate * jax.nn.sigmoid(gate) * val).astype(BF16)

    def pair(pi, carry):
        f0 = 2 * pi
        a0 = activation(f0, 0)
        a1 = activation(f0 + 1, 1)
        o_ref[0] += _dot(a0, wd_ref[f0]) + _dot(a1, wd_ref[f0 + 1])
        return carry

    lax.fori_loop(0, nf // 2, pair, 0)
    if nf % 2:
        o_ref[0] += _dot(activation(nf - 1, 0), wd_ref[nf - 1])
    if final:
        o_ref[0] = _rms(o_ref[0], gf_ref[...])


def _conv_ffn(x3, g, w_up, conv_w, conv_b, w_down, final_gain=None):
    b, s, d = x3.shape
    ffn = w_down.shape[0]
    tm, tf = FFN_ROWS, FFN_COLS
    nf = ffn // tf
    halo = CONV_HALO
    final = final_gain is not None
    w_up_t = w_up.reshape(d, 2, nf, tf).transpose(1, 2, 0, 3)
    conv_w_t = conv_w.reshape(CONV_WIDTH, 2, nf, tf).transpose(1, 2, 0, 3)
    conv_b_t = conv_b.reshape(2, nf, 1, tf)
    w_down_t = w_down.reshape(nf, tf, d)
    in_specs = [
        pl.BlockSpec((1, tm, d), lambda bi, i: (bi, i, 0)),
        pl.BlockSpec((1, halo, d), lambda bi, i: (bi, jnp.maximum(i * (tm // halo) - 1, 0), 0)),
        _const_spec((1, d)),
        _const_spec(w_up_t.shape),
        _const_spec(conv_w_t.shape),
        _const_spec(conv_b_t.shape),
        _const_spec(w_down_t.shape),
    ]
    args = [x3, x3, g.reshape(1, d), w_up_t, conv_w_t, conv_b_t, w_down_t]
    if final:
        in_specs.append(_const_spec((1, d)))
        args.append(final_gain.reshape(1, d))
    return pl.pallas_call(
        functools.partial(_ffn_kernel, tm=tm, final=final),
        out_shape=jax.ShapeDtypeStruct((b, s, d), F32),
        grid=(b, s // tm),
        in_specs=in_specs,
        out_specs=pl.BlockSpec((1, tm, d), lambda bi, i: (bi, i, 0)),
        scratch_shapes=[
            pltpu.VMEM((tm + halo, d), BF16),
            pltpu.VMEM((2, 2, tm + halo, tf), F32),
        ],
        compiler_params=_params("parallel", "parallel"),
    )(*args)


def _nsa_front_kernel(x_ref, gkv_ref, gq_ref, wkv_ref, wq_ref, wg_ref, cos_ref, sin_ref,
                      kv_ref, q_ref, qr_ref, gt_ref, *, q_scale):
    x = x_ref[...]
    y = x * lax.rsqrt(jnp.mean(x * x, axis=-1, keepdims=True) + EPS)
    hn_kv = (y * gkv_ref[...]).astype(BF16)
    hn_q = (y * gq_ref[...]).astype(BF16)
    cos, sin = cos_ref[...], sin_ref[...]
    rope_w = cos.shape[1]

    for grp in range(wkv_ref.shape[1] // rope_w):
        acc = _dot(hn_kv, wkv_ref[:, grp * rope_w:(grp + 1) * rope_w])
        if grp in (2, 4):
            acc = _rope_tile(acc, cos, sin)
        kv_ref[:, grp * rope_w:(grp + 1) * rope_w] = acc.astype(kv_ref.dtype)
    for grp in range(wq_ref.shape[1] // rope_w):
        acc = _dot(hn_q, wq_ref[:, grp * rope_w:(grp + 1) * rope_w]) * q_scale
        q_ref[0, grp * rope_w:(grp + 1) * rope_w, :] = acc.T.astype(q_ref.dtype)
        qr_ref[0, grp * rope_w:(grp + 1) * rope_w, :] = _rope_tile(acc, cos, sin).T.astype(qr_ref.dtype)
    gt_ref[0] = jax.nn.sigmoid(_dot(hn_q, wg_ref[...])).T


def _nsa_front(x2, g_kv, g_q, w_kv, w_q, w_g, rope, *, seq, q_scale):
    n, d = x2.shape
    tm = PROJ_ROWS
    per_seq = seq // tm
    batch = n // seq
    rope_w = rope[0].shape[1]
    row = lambda m: pl.BlockSpec((tm, m), lambda i: (i, 0))
    col = lambda m: pl.BlockSpec((1, m, tm), lambda i: (i // per_seq, 0, i % per_seq))
    return pl.pallas_call(
        functools.partial(_nsa_front_kernel, q_scale=q_scale),
        out_shape=[
            jax.ShapeDtypeStruct((n, w_kv.shape[1]), BF16),
            jax.ShapeDtypeStruct((batch, w_q.shape[1], seq), BF16),
            jax.ShapeDtypeStruct((batch, w_q.shape[1], seq), BF16),
            jax.ShapeDtypeStruct((batch, w_g.shape[1], seq), F32),
        ],
        grid=(n // tm,),
        in_specs=[
            row(d), _const_spec((1, d)), _const_spec((1, d)),
            _const_spec(w_kv.shape), _const_spec(w_q.shape), _const_spec(w_g.shape),
            pl.BlockSpec((tm, rope_w), lambda i: (i % per_seq, 0)),
            pl.BlockSpec((tm, rope_w), lambda i: (i % per_seq, 0)),
        ],
        out_specs=[row(w_kv.shape[1]), col(w_q.shape[1]), col(w_q.shape[1]), col(w_g.shape[1])],
        compiler_params=_params("parallel"),
    )(x2, g_kv.reshape(1, d), g_q.reshape(1, d), w_kv, w_q, w_g, *rope)


def _compress_kernel(tk_ref, tv_ref, w1k_ref, w1v_ref, bk_ref, bv_ref, w2k_ref, w2v_ref, o_ref, *, rows_per_seq):
    rows = tk_ref.shape[0]
    row = lax.broadcasted_iota(jnp.int32, (rows, 1), 0)
    valid = jnp.where((row & (rows_per_seq - 1)) < rows_per_seq - 1, 1.0, 0.0)

    def one(t_ref, w1_ref, b_ref, w2_ref):
        y = _dot(t_ref[...], w1_ref[...])
        hid = y.shape[1] // 2
        pre = y[:, :hid] + pltpu.roll(y[:, hid:], rows - 1, 0) + b_ref[0:1, :]
        out = _dot(jax.nn.gelu(pre).astype(BF16), w2_ref[...])
        return out * valid

    o_ref[...] = jnp.concatenate(
        [one(tk_ref, w1k_ref, bk_ref, w2k_ref), one(tv_ref, w1v_ref, bv_ref, w2v_ref)], axis=1
    ).astype(o_ref.dtype)


def _pe_bias_kernel(pk_ref, pv_ref, w1k_ref, w1v_ref, ok_ref, ov_ref):
    ok_ref[...] = _dot(pk_ref[...], w1k_ref[...])
    ov_ref[...] = _dot(pv_ref[...], w1v_ref[...])


def _compress(tk, tv, pe_k, pe_v, w1k, w1v, w2k, w2v, *, rows_per_seq):
    r, kdim = tk.shape
    hid = w1k.shape[1]
    dh = w2k.shape[1]
    tr = CMP_ROWS
    sub = 8
    pk = jnp.broadcast_to(pe_k.reshape(1, -1), (sub, 2 * kdim)).astype(BF16)
    pv = jnp.broadcast_to(pe_v.reshape(1, -1), (sub, 2 * kdim)).astype(BF16)
    bk, bv = pl.pallas_call(
        _pe_bias_kernel,
        out_shape=[jax.ShapeDtypeStruct((sub, hid), F32)] * 2,
    )(pk, pv, w1k, w1v)
    w1k_cat = jnp.concatenate([w1k[:kdim], w1k[kdim:]], axis=1)
    w1v_cat = jnp.concatenate([w1v[:kdim], w1v[kdim:]], axis=1)
    return pl.pallas_call(
        functools.partial(_compress_kernel, rows_per_seq=rows_per_seq),
        out_shape=jax.ShapeDtypeStruct((r, 2 * dh), BF16),
        grid=(r // tr,),
        in_specs=[
            pl.BlockSpec((tr, kdim), lambda i: (i, 0)),
            pl.BlockSpec((tr, kdim), lambda i: (i, 0)),
            _const_spec((kdim, 2 * hid)),
            _const_spec((kdim, 2 * hid)),
            _const_spec((sub, hid)),
            _const_spec((sub, hid)),
            _const_spec((hid, dh)),
            _const_spec((hid, dh)),
        ],
        out_specs=pl.BlockSpec((tr, 2 * dh), lambda i: (i, 0)),
        compiler_params=_params("parallel"),
    )(tk, tv, w1k_cat, w1v_cat, bk, bv, w2k, w2v)


def _cmp_to_sel_weights_t(n_cmp_pad, n_sel):
    c0 = np.arange(n_cmp_pad)[None, :] * CMP_STRIDE
    s0 = np.arange(n_sel)[:, None] * SEL_BLOCK
    ov = np.clip(np.minimum(c0 + CMP_BLOCK, s0 + SEL_BLOCK) - np.maximum(c0, s0), 0, None)
    return (ov / CMP_BLOCK).astype(np.float32)


def _nsa_kernel(q_ref, qr_ref, gt_ref, kc_ref, ks_ref, vs_ref, kw_ref, vw_ref, wt_ref, oh_ref, o_ref,
                ksel_s, kwin_s, vsel_s, vwin_s, vcmp_s, m_s, acc_s, *, tq, tk):
    grp, dh = NSA_GROUP, NSA_DH
    heads = 2 * grp
    qi = pl.program_id(2)
    t0 = qi * tq
    n_sel = wt_ref.shape[0]
    n_cmp_pad = wt_ref.shape[1]
    seq = ks_ref.shape[1]

    @pl.when(qi == 0)
    def _():
        ones_row = jnp.where(lax.broadcasted_iota(jnp.int32, (BF16_SUBLANES, tk), 0) == 0, 1.0, 0.0).astype(BF16)
        for h2 in range(2):
            cols = slice(h2 * dh, (h2 + 1) * dh)
            ksel_s[h2, :, 0:dh] = ks_ref[0, :, cols]
            ksel_s[h2, :, dh:2 * dh] = oh_ref[...]
            kwin_s[h2, :, 0:dh] = kw_ref[0, :, cols]
            kwin_s[h2, :, dh:2 * dh] = jnp.zeros((seq, dh), BF16)
            vcmp_s[h2] = kc_ref[0, h2, :, dh:2 * dh].astype(F32).T.astype(BF16)

        def build(j, carry):
            r0 = pl.multiple_of(j * tk, tk)
            for src, dst in ((vs_ref, vsel_s), (vw_ref, vwin_s)):
                blk_t = src[0, pl.ds(r0, tk), :].astype(F32).T
                for h2 in range(2):
                    dst[h2, j, 0:dh, :] = blk_t[h2 * dh:(h2 + 1) * dh].astype(BF16)
                    dst[h2, j, dh:, :] = ones_row
            return carry

        lax.fori_loop(0, seq // tk, build, 0)

    tpos = t0 + lax.broadcasted_iota(jnp.int32, (1, tq), 1)
    key_iota = lax.broadcasted_iota(jnp.int32, (tk, 1), 0)
    cmp_end = lax.broadcasted_iota(jnp.int32, (n_cmp_pad, 1), 0) * CMP_STRIDE + (CMP_BLOCK - 1)
    cmp_mask = cmp_end <= tpos
    blk = lax.broadcasted_iota(jnp.int32, (n_sel, 1), 0)
    cur = tpos >> 6
    forced = (blk == 0) | (blk == cur) | (blk == cur - 1)
    win_lo = jnp.maximum(tpos - WINDOW, -1)

    def run_tiles(tiles, q_augs):
        chunks = []
        for sub in range(tq // tk):
            for branch, k_s, v_s, h2, tile, mode in tiles:
                kt = jnp.maximum(tile, 0) * (tq // tk) + sub
                k0 = pl.multiple_of(kt * tk, tk)
                shared = dict(k=k_s[h2, pl.ds(k0, tk), :], v=v_s[h2, kt], mode=mode,
                              kpos=tile * tq + sub * tk + key_iota)
                for g in range(grp):
                    chunks.append(dict(shared, slot=(h2 * 2 + branch) * grp + g, q=q_augs[h2][g]))

        def scores(c):
            s = _dot(c["k"], c["q"])
            if c["mode"] == "causal":
                s = jnp.where(c["kpos"] <= tpos, s, NEG_INF)
            elif c["mode"] == "window_start":
                s = jnp.where(c["kpos"] > win_lo, s, NEG_INF)
            elif c["mode"] == "in_sequence":
                s = jnp.where(c["kpos"] >= 0, s, NEG_INF)
            c["s"] = s

        def softmax_update(c):
            m = m_s[c["slot"]]
            m_new = jnp.maximum(m, jnp.max(c["s"], axis=0, keepdims=True))
            c["p"] = jnp.exp2(c.pop("s") - m_new).astype(BF16)
            c["alpha"] = jnp.exp2(m - m_new)
            m_s[c["slot"]] = m_new

        def accumulate(c):
            acc_s[c["slot"]] = c["alpha"] * acc_s[c["slot"]] + _dot(c["v"], c.pop("p"))

        ahead = 2
        for c in chunks[:ahead]:
            scores(c)
        for i, c in enumerate(chunks):
            softmax_update(c)
            if i + ahead < len(chunks):
                scores(chunks[i + ahead])
            accumulate(c)

    def result(branch, h2, g):
        slot = (h2 * 2 + branch) * grp + g
        return acc_s[slot, 0:dh, :] / acc_s[slot, dh:dh + 1, :]

    m_s[...] = jnp.full(m_s.shape, NEG_INF, F32)
    acc_s[...] = jnp.zeros(acc_s.shape, F32)
    q_augs, o_cmps = [], []
    cmp_scores = [[_dot(kc_ref[0, h2, :, 0:dh], q_ref[0, (h2 * grp + g) * dh:(h2 * grp + g + 1) * dh, :])
                   for g in range(grp)] for h2 in range(2)]

    for h2 in range(2):
        head_rows = lambda ref, g: ref[0, (h2 * grp + g) * dh:(h2 * grp + g + 1) * dh, :]

        p_grp = jnp.zeros((n_cmp_pad, tq), F32)
        o_cmp = []
        for g in range(grp):
            s = jnp.where(cmp_mask, cmp_scores[h2][g], NEG_INF)
            e = jnp.where(cmp_mask, jnp.exp2(s - jnp.max(s, axis=0, keepdims=True)), 0.0)
            inv = 1.0 / jnp.maximum(jnp.sum(e, axis=0, keepdims=True), 1e-30)
            o_cmp.append(_dot(vcmp_s[h2], e.astype(BF16)) * inv)
            p_grp = p_grp + e * inv
        i3 = _dot(wt_ref[...], jnp.concatenate(_split3(p_grp), axis=1))
        imp = i3[:, 0:tq] + i3[:, tq:2 * tq] + i3[:, 2 * tq:]
        imp = jnp.where(blk > cur, -1.0, jnp.where(forced, FORCE_SCORE, imp))
        rows = 8
        ranks = []
        for r0 in range(0, n_sel, rows):
            mine = imp[r0:r0 + rows]
            row_id = blk[r0:r0 + rows]
            rank = jnp.zeros((rows, tq), F32)
            for jp in range(n_sel):
                other = imp[jp:jp + 1, :]
                ge = lambda: jnp.where(other >= mine, 1.0, 0.0)
                gt_ = lambda: jnp.where(other > mine, 1.0, 0.0)
                if jp < r0:
                    rank = rank + ge()
                elif jp >= r0 + rows:
                    rank = rank + gt_()
                else:
                    rank = rank + jnp.where(row_id > jp, ge(), gt_())
            ranks.append(rank)
        rank = jnp.concatenate(ranks, axis=0)
        sel_bias = jnp.where(rank < float(SEL_TOPK), 0.0, NEG_INF).astype(BF16)
        pad = jnp.zeros((dh - n_sel, tq), BF16)
        q_augs.append([jnp.concatenate([head_rows(qr_ref, g), sel_bias, pad], axis=0) for g in range(grp)])
        o_cmps.append(o_cmp)

    sel, win = 0, 1

    def sel_tiles(kts):
        return [(sel, ksel_s, vsel_s, h2, kt, None) for kt in kts for h2 in range(2)]

    def sel_body(pair, carry):
        run_tiles(sel_tiles([2 * pair, 2 * pair + 1]), q_augs)
        return carry

    lax.fori_loop(0, qi // 2, sel_body, 0)

    @pl.when(qi % 2 == 1)
    def _():
        run_tiles(sel_tiles([qi - 1]), q_augs)

    run_tiles([(sel, ksel_s, vsel_s, h2, qi, "causal") for h2 in range(2)]
              + [(win, kwin_s, vwin_s, h2, qi - 2, "window_start") for h2 in range(2)]
              + [(win, kwin_s, vwin_s, h2, qi - 1, "in_sequence") for h2 in range(2)]
              + [(win, kwin_s, vwin_s, h2, qi, "causal") for h2 in range(2)], q_augs)

    for h2 in range(2):
        for g in range(grp):
            hg = h2 * grp + g
            gate = lambda branch: gt_ref[0, branch * heads + hg:branch * heads + hg + 1, :]
            o_ref[0, hg * dh:(hg + 1) * dh, :] = (
                gate(0) * o_cmps[h2][g] + gate(1) * result(sel, h2, g) + gate(2) * result(win, h2, g)
            ).astype(o_ref.dtype)


def _nsa_attention(q_t, qr_t, gates_t, kvc, kv):
    b, d, s = q_t.shape
    tq, tk = NSA_TQ, NSA_TK
    assert WINDOW == 2 * tq and tq % SEL_BLOCK == 0 and tq % tk == 0
    dh = NSA_DH
    pairs = NSA_KV_HEADS // 2
    heads = 2 * NSA_GROUP
    pw = heads * dh
    n_cmp_pad = kvc.shape[2]
    n_sel = s // SEL_BLOCK
    wt = jnp.asarray(_cmp_to_sel_weights_t(n_cmp_pad, n_sel), BF16)
    onehot = np.zeros((s, dh), np.float32)
    onehot[np.arange(s), np.arange(s) // SEL_BLOCK] = 1.0
    kv_spec = lambda c6: pl.BlockSpec((1, s, LANE), lambda bi, p, i: (bi, 0, 2 * c6 + p))
    return pl.pallas_call(
        functools.partial(_nsa_kernel, tq=tq, tk=tk),
        out_shape=jax.ShapeDtypeStruct((b, d, s), BF16),
        grid=(b, pairs, s // tq),
        in_specs=[
            pl.BlockSpec((1, pw, tq), lambda bi, p, i: (bi, p, i)),
            pl.BlockSpec((1, pw, tq), lambda bi, p, i: (bi, p, i)),
            pl.BlockSpec((1, LANE, tq), lambda bi, p, i: (bi, p, i)),
            pl.BlockSpec((1, 2, n_cmp_pad, LANE), lambda bi, p, i: (bi, p, 0, 0)),
            kv_spec(2), kv_spec(3), kv_spec(4), kv_spec(5),
            _const_spec(wt.shape),
            _const_spec(onehot.shape),
        ],
        out_specs=pl.BlockSpec((1, pw, tq), lambda bi, p, i: (bi, p, i)),
        scratch_shapes=[
            pltpu.VMEM((2, s, 2 * dh), BF16),
            pltpu.VMEM((2, s, 2 * dh), BF16),
            pltpu.VMEM((2, s // tk, NSA_ACC_ROWS, tk), BF16),
            pltpu.VMEM((2, s // tk, NSA_ACC_ROWS, tk), BF16),
            pltpu.VMEM((2, dh, n_cmp_pad), BF16),
            pltpu.VMEM((2 * heads, 1, tq), F32),
            pltpu.VMEM((2 * heads, NSA_ACC_ROWS, tq), F32),
        ],
        compiler_params=_params("parallel", "parallel", "arbitrary"),
    )(q_t, qr_t, gates_t, kvc, kv, kv, kv, kv, wt, jnp.asarray(onehot, BF16))


def _rope_tables(seq, width):
    half = NSA_DH // 2
    inv = ROPE_THETA ** (-jnp.arange(half, dtype=F32) / half)
    ang = jnp.arange(seq, dtype=F32)[:, None] * inv[None, :]
    cos, sin = jnp.cos(ang), jnp.sin(ang)
    reps = width // NSA_DH
    cos_full = jnp.tile(jnp.concatenate([cos, cos], axis=1), (1, reps))
    sin_signed = jnp.tile(jnp.concatenate([-sin, sin], axis=1), (1, reps))
    return cos_full, sin_signed


def _nsa_gate_weights(w_gate):
    d = w_gate.shape[0]
    pairs = NSA_KV_HEADS // 2
    heads = 2 * NSA_GROUP
    wg = w_gate.reshape(d, 3, pairs, heads).transpose(0, 2, 1, 3).reshape(d, pairs, 3 * heads)
    return jnp.pad(wg, ((0, 0), (0, 0), (0, LANE - 3 * heads))).reshape(d, pairs * LANE)


def kernel(x, norm_mix, norm_ffn, gla_w_in, gla_w_alpha_up, gla_b_alpha, gla_norm, gla_w_o, kv_norm, nsa_w_kv, cmp_pe_k, cmp_pe_v, cmp_k_w1, cmp_k_w2, cmp_v_w1, cmp_v_w2, nsa_w_in, nsa_w_o, ffn_w_up, ffn_conv_w, ffn_conv_b, ffn_w_down, norm_final):
    b, s, d = x.shape
    n = b * s
    x2 = x.reshape(n, d)

    n_main = 2 * GLA_HEADS * GLA_DK + 2 * GLA_HEADS * GLA_DV
    w_in = gla_w_in[0]
    n_pairs = len(GATE_PIECE_PAIRS)
    lane_pad = LANE - n_pairs * GLA_GATE_RANK
    w_low = jnp.pad(jnp.tile(w_in[:, n_main:], (1, n_pairs)), ((0, 0), (0, lane_pad))).astype(BF16)
    w_up_pieces = _split3(gla_w_alpha_up[0])
    w_up6 = jnp.pad(jnp.concatenate([w_up_pieces[iw] for _, iw in GATE_PIECE_PAIRS], axis=0),
                    ((0, lane_pad), (0, 0)))
    proj, log_a = _gla_front(x2, norm_mix[0], w_in[:, :n_main].astype(BF16), w_low, w_up6, gla_b_alpha[0])
    o = _gla_core(proj.reshape(b, s, n_main), log_a.reshape(b, s, -1), gla_norm[0])
    x2 = _proj_res(o.reshape(n, -1), gla_w_o[0].astype(BF16), x2)
    x2 = _conv_ffn(x2.reshape(b, s, d), norm_ffn[0], ffn_w_up[0].astype(BF16), ffn_conv_w[0], ffn_conv_b[0],
                   ffn_w_down[0].astype(BF16)).reshape(n, d)

    hk, dh = NSA_KV_HEADS, NSA_DH
    nq = NSA_HEADS * dh
    w_nsa = nsa_w_in[0]
    kv, q_t, qr_t, gates_t = _nsa_front(
        x2, kv_norm, norm_mix[1], nsa_w_kv.astype(BF16), w_nsa[:, :nq].astype(BF16),
        _nsa_gate_weights(w_nsa[:, nq:]).astype(BF16), _rope_tables(s, 4 * dh), seq=s, q_scale=dh ** -0.5 * LOG2E)
    kv4 = kv.reshape(b, s, 6, hk, dh)
    rows_per_seq = s // CMP_STRIDE

    def token_groups(t):
        return t.transpose(0, 2, 1, 3).reshape(b * hk * rows_per_seq, CMP_STRIDE * dh)

    kvc = _compress(token_groups(kv4[:, :, 0]), token_groups(kv4[:, :, 1]), cmp_pe_k, cmp_pe_v,
                    cmp_k_w1.astype(BF16), cmp_v_w1.astype(BF16), cmp_k_w2.astype(BF16),
                    cmp_v_w2.astype(BF16), rows_per_seq=rows_per_seq)
    kvc = kvc.reshape(b, hk, rows_per_seq, 2 * dh)

    o_t = _nsa_attention(q_t, qr_t, gates_t, kvc, kv.reshape(b, s, -1))
    x2 = _proj_res(o_t, nsa_w_o[0].astype(BF16), x2, transposed=True)
    return _conv_ffn(x2.reshape(b, s, d), norm_ffn[1], ffn_w_up[1].astype(BF16), ffn_conv_w[1], ffn_conv_b[1],
                     ffn_w_down[1].astype(BF16), norm_final)
```

```python
import functools
import math

import numpy as np
import jax
import jax.numpy as jnp
from jax import lax
from jax.experimental import pallas as pl
from jax.experimental.pallas import tpu as pltpu

F32 = jnp.float32
BF16 = jnp.bfloat16

EPS = 1e-6
NEG_INF = -1e30
ROPE_THETA = 10000.0
LOG2E = math.log2(math.e)

LANE = 128
BF16_SUBLANES = 16
VMEM_LIMIT_BYTES = 56 * 1024 * 1024

GLA_HEADS = 4
GLA_DK = 128
GLA_DV = 256
GLA_GATE_RANK = 16
GLA_TAU = 16.0
GLA_CHUNK = 64
GLA_LEVEL_GROUP = 4
GATE_PIECE_PAIRS = ((0, 0), (0, 1), (1, 0), (0, 2), (1, 1), (2, 0))
NSA_HEADS = 16
NSA_KV_HEADS = 4
NSA_GROUP = 4
NSA_DH = 64
CMP_BLOCK = 32
CMP_STRIDE = 16
CMP_HIDDEN = 256
SEL_BLOCK = 64
SEL_TOPK = 16
WINDOW = 512
FORCE_SCORE = 1e4
CONV_WIDTH = 3
CONV_HALO = 16

PROJ_ROWS = 1024
PROJ_COLS = 512
RES_ROWS = 1024
GLA_ROWS = 1024
FFN_ROWS = 1024
FFN_COLS = 256
FFN_TILES_PER_TRIP = 4
CMP_ROWS = 512
NSA_TQ = 256
NSA_TK = 256
NSA_ACC_ROWS = NSA_DH + BF16_SUBLANES


def _dot(a, b):
    return jnp.dot(a, b, preferred_element_type=F32)


def _dot_nt(a, b):
    return lax.dot_general(a, b, (((1,), (1,)), ((), ())), preferred_element_type=F32)


def _dot_tn(a, b):
    return lax.dot_general(a, b, (((0,), (0,)), ((), ())), preferred_element_type=F32)


def _rms(x, g):
    return x * lax.rsqrt(jnp.mean(x * x, axis=-1, keepdims=True) + EPS) * g


def _split3(x):
    hi = x.astype(BF16)
    r1 = x - hi.astype(F32)
    mid = r1.astype(BF16)
    lo = (r1 - mid.astype(F32)).astype(BF16)
    return hi, mid, lo


def _rope_tile(x, cos, sin_signed):
    w = x.shape[-1]
    lane = lax.broadcasted_iota(jnp.int32, x.shape, 1)
    first_half = (lane & (NSA_DH - 1)) < (NSA_DH // 2)
    partner = jnp.where(first_half, pltpu.roll(x, w - NSA_DH // 2, 1), pltpu.roll(x, NSA_DH // 2, 1))
    return x * cos + partner * sin_signed


def _params(*sem):
    return pltpu.CompilerParams(dimension_semantics=sem, vmem_limit_bytes=VMEM_LIMIT_BYTES)


def _const_spec(shape):
    nd = len(shape)
    return pl.BlockSpec(shape, lambda *_: (0,) * nd, pipeline_mode=pl.Buffered(1))


def _gla_front_kernel(x_ref, g_ref, w_ref, wl_ref, wu_ref, b_ref, o_ref, la_ref):
    hn = _rms(x_ref[...], g_ref[...]).astype(BF16)
    a_rep = _dot(hn, wl_ref[...])
    hi, mid, lo = _split3(a_rep)
    group = lax.broadcasted_iota(jnp.int32, a_rep.shape, 1) // GLA_GATE_RANK
    a_piece = [g for g, (ia, _) in enumerate(GATE_PIECE_PAIRS)]
    is_piece = lambda p: functools.reduce(
        jnp.logical_or, [group == g for g in a_piece if GATE_PIECE_PAIRS[g][0] == p])
    a6 = jnp.where(is_piece(2), lo, jnp.where(is_piece(1), mid, hi))
    z = b_ref[...] + _dot(a6, wu_ref[...])
    log_sig = jnp.minimum(z, 0.0) - jnp.log(1.0 + jnp.exp(-jnp.abs(z)))
    la_ref[...] = log_sig * (1.0 / GLA_TAU)
    m = w_ref.shape[1]
    for c in range(0, m, PROJ_COLS):
        o_ref[:, c:c + PROJ_COLS] = _dot(hn, w_ref[:, c:c + PROJ_COLS]).astype(o_ref.dtype)


def _gla_front(x2, g, w_main, w_low, w_up6, b):
    n, d = x2.shape
    m = w_main.shape[1]
    hk = w_up6.shape[-1]
    tm = PROJ_ROWS
    return pl.pallas_call(
        _gla_front_kernel,
        out_shape=[jax.ShapeDtypeStruct((n, m), BF16), jax.ShapeDtypeStruct((n, hk), F32)],
        grid=(n // tm,),
        in_specs=[
            pl.BlockSpec((tm, d), lambda i: (i, 0)),
            _const_spec((1, d)),
            _const_spec((d, m)),
            _const_spec((d, LANE)),
            _const_spec((LANE, hk)),
            _const_spec((1, hk)),
        ],
        out_specs=[pl.BlockSpec((tm, m), lambda i: (i, 0)), pl.BlockSpec((tm, hk), lambda i: (i, 0))],
        compiler_params=_params("parallel"),
    )(x2, g.reshape(1, d), w_main, w_low, w_up6, b.reshape(1, hk))


def _gla_constants():
    c = GLA_CHUNK
    idx = np.arange(c)
    row, col = idx[:, None], idx[None, :]
    mats = [col <= row, col > row]
    masks = []
    h = 1
    while h < c:
        blk = idx // (2 * h)
        upper = (idx % (2 * h)) >= h
        r = (blk * 2 * h + h - 1)[:, None]
        mats.append(np.where(upper[:, None], (col > r) & (col <= row), (col > row) & (col <= r)))
        masks.append((blk[:, None] == blk[None, :]) & upper[:, None] & (~upper[None, :]))
        h *= 2
    masks.append(np.eye(c, dtype=bool))
    strips = np.zeros((len(masks), c, LANE), np.float32)
    for lvl, m in enumerate(masks):
        half = (lvl % GLA_LEVEL_GROUP) % (LANE // c)
        strips[lvl, :, half * c:(half + 1) * c] = m
    return np.concatenate(mats, 0).astype(np.float32), strips


def _gla_kernel(q_ref, k_ref, v_ref, r_ref, g_ref, gs_ref, mk_ref, gn_ref, o_ref, st_ref, *, n_chunks):
    c, dk, dv = GLA_CHUNK, GLA_DK, GLA_DV
    n_levels = mk_ref.shape[0]
    per_strip = LANE // c
    q_scale = dk ** -0.5

    @pl.when(pl.program_id(1) == 0)
    def _():
        st_ref[...] = jnp.zeros_like(st_ref)

    def decays(h, r0, t):
        ks = slice(h * dk, (h + 1) * dk)
        t["q"] = q_ref[0, pl.ds(r0, c), ks].astype(F32) * q_scale
        t["k"] = k_ref[0, pl.ds(r0, c), ks].astype(F32)
        t["v"] = v_ref[0, pl.ds(r0, c), h * dv:(h + 1) * dv]
        hi, mid, _ = _split3(g_ref[0, pl.ds(r0, c), ks])
        t["e2"] = _dot(gs_ref[...], jnp.concatenate([hi, mid], axis=1))

    def level_products(t):
        e2 = t.pop("e2")
        x = t["x"] = jnp.exp(e2[:, :dk] + e2[:, dk:])
        q, k = t["q"], t["k"]
        t["prods"] = []
        for first in range(0, n_levels, GLA_LEVEL_GROUP):
            lvls = list(range(first, min(first + GLA_LEVEL_GROUP, n_levels)))
            factor = lambda lvl: x[(2 + lvl) * c:(3 + lvl) * c] if lvl < n_levels - 1 else 1.0
            q_stack = jnp.concatenate([(q * factor(lvl)).astype(BF16) for lvl in lvls], axis=0)
            k_stack = jnp.concatenate([(k * factor(lvl)).astype(BF16) for lvl in lvls]
                                      + [jnp.zeros((c, dk), BF16)] * (GLA_LEVEL_GROUP - len(lvls)), axis=0)
            t["prods"].append((lvls, _dot_nt(q_stack, k_stack)))
        t["q_inter"] = (q * x[0:c]).astype(BF16)
        t["k_state"] = (k * x[c:2 * c]).astype(BF16)

    def outputs(h, t):
        strip = jnp.zeros((c, LANE), F32)
        for lvls, prod in t.pop("prods"):
            for j, lvl in enumerate(lvls):
                col = (j // per_strip) * LANE
                strip = strip + prod[j * c:(j + 1) * c, col:col + LANE] * mk_ref[lvl]
        att = strip[:, 0:c]
        for part in range(1, per_strip):
            att = att + strip[:, part * c:(part + 1) * c]
        st = st_ref[h]
        t["o"] = _dot(att.astype(BF16), t["v"]) + _dot_nt(t["q_inter"], st.astype(BF16))
        st_ref[h] = st * t["x"][c - 1:c] + _dot_tn(t["v"], t["k_state"])

    def finish(h, r0, t):
        o = t["o"]
        o = o * lax.rsqrt(jnp.mean(o * o, axis=-1, keepdims=True) + EPS) * gn_ref[h:h + 1, :]
        r = r_ref[0, pl.ds(r0, c), h * dv:(h + 1) * dv].astype(F32)
        o_ref[0, pl.ds(r0, c), h * dv:(h + 1) * dv] = (o * (r * jax.nn.sigmoid(r))).astype(o_ref.dtype)

    def body(ci, carry):
        r0 = pl.multiple_of(ci * c, c)
        heads = [dict() for _ in range(GLA_HEADS)]
        for h, t in enumerate(heads):
            decays(h, r0, t)
        for t in heads:
            level_products(t)
        for h, t in enumerate(heads):
            outputs(h, t)
        for h, t in enumerate(heads):
            finish(h, r0, t)
        return carry

    lax.fori_loop(0, n_chunks, body, 0)


def _gla_core(proj, log_a, gla_norm):
    b, s, _ = proj.shape
    h, dk, dv = GLA_HEADS, GLA_DK, GLA_DV
    ts = GLA_ROWS
    gs, mk = _gla_constants()
    return pl.pallas_call(
        functools.partial(_gla_kernel, n_chunks=ts // GLA_CHUNK),
        out_shape=jax.ShapeDtypeStruct((b, s, h * dv), BF16),
        grid=(b, s // ts),
        in_specs=[
            pl.BlockSpec((1, ts, h * dk), lambda i, j: (i, j, 0)),
            pl.BlockSpec((1, ts, h * dk), lambda i, j: (i, j, 1)),
            pl.BlockSpec((1, ts, h * dv), lambda i, j: (i, j, 1)),
            pl.BlockSpec((1, ts, h * dv), lambda i, j: (i, j, 2)),
            pl.BlockSpec((1, ts, h * dk), lambda i, j: (i, j, 0)),
            _const_spec(gs.shape),
            _const_spec(mk.shape),
            _const_spec((h, dv)),
        ],
        out_specs=pl.BlockSpec((1, ts, h * dv), lambda i, j: (i, j, 0)),
        scratch_shapes=[pltpu.VMEM((h, dv, dk), F32)],
        compiler_params=_params("parallel", "arbitrary"),
    )(proj, proj, proj, proj, log_a, jnp.asarray(gs, BF16), jnp.asarray(mk), gla_norm)


def _proj_res_kernel(a_ref, w_ref, r_ref, o_ref):
    o_ref[...] = r_ref[...] + _dot(a_ref[...], w_ref[...])


def _proj_res_t_kernel(a_ref, w_ref, r_ref, o_ref):
    o_ref[...] = r_ref[...] + _dot_tn(a_ref[0], w_ref[...])


def _proj_res(a, w, res, *, transposed=False):
    n, d = res.shape
    k = w.shape[0]
    tm = RES_ROWS
    if transposed:
        per_seq = a.shape[2] // tm
        a_spec = pl.BlockSpec((1, k, tm), lambda i: (i // per_seq, 0, i % per_seq))
    else:
        a_spec = pl.BlockSpec((tm, k), lambda i: (i, 0))
    return pl.pallas_call(
        _proj_res_t_kernel if transposed else _proj_res_kernel,
        out_shape=jax.ShapeDtypeStruct((n, d), F32),
        grid=(n // tm,),
        in_specs=[
            a_spec,
            _const_spec((k, d)),
            pl.BlockSpec((tm, d), lambda i: (i, 0)),
        ],
        out_specs=pl.BlockSpec((tm, d), lambda i: (i, 0)),
        compiler_params=_params("parallel"),
    )(a, w, res)


def _ffn_kernel(*refs, tm, final):
    if final:
        xm_ref, xh_ref, gn_ref, wup_ref, cw_ref, cb_ref, wd_ref, gf_ref, o_ref, hn_ref, u_ref = refs
    else:
        xm_ref, xh_ref, gn_ref, wup_ref, cw_ref, cb_ref, wd_ref, o_ref, hn_ref, u_ref = refs
    halo = CONV_HALO
    nf = wd_ref.shape[0]
    g = gn_ref[...]
    hn_ref[0:halo, :] = _rms(xh_ref[0], g).astype(BF16)
    hn_ref[halo:, :] = _rms(xm_ref[0], g).astype(BF16)
    o_ref[0] = xm_ref[0]
    keep = jnp.where(pl.program_id(1) > 0, 1.0, 0.0)

    def up_project(fi, slot):
        hn = hn_ref[...]
        for gv in range(2):
            u = _dot(hn, wup_ref[gv, fi])
            u_ref[slot, gv] = u
            u_ref[slot, gv, 0:halo, :] = u[0:halo] * keep

    def activation(fi, slot):
        outs = []
        for gv in range(2):
            out = cb_ref[gv, fi]
            for tap in range(CONV_WIDTH):
                start = halo - (CONV_WIDTH - 1) + tap
                out = out + u_ref[slot, gv, start:start + tm, :] * cw_ref[gv, fi, tap:tap + 1, :]
            outs.append(out)
        gate, val = outs
        return (gate * jax.nn.sigmoid(gate) * val).astype(BF16)

    def run(tiles):
        up_project(tiles[0], 0)
        for j, fi in enumerate(tiles):
            if j + 1 < len(tiles):
                up_project(tiles[j + 1], (j + 1) % 2)
            o_ref[0] += _dot(activation(fi, j % 2), wd_ref[fi])

    group = FFN_TILES_PER_TRIP

    def trip(ti, carry):
        run([ti * group + j for j in range(group)])
        return carry

    lax.fori_loop(0, nf // group, trip, 0)
    if nf % group:
        run(list(range(nf - nf % group, nf)))
    if final:
        o_ref[0] = _rms(o_ref[0], gf_ref[...])


def _conv_ffn(x3, g, w_up, conv_w, conv_b, w_down, final_gain=None):
    b, s, d = x3.shape
    ffn = w_down.shape[0]
    tm, tf = FFN_ROWS, FFN_COLS
    nf = ffn // tf
    halo = CONV_HALO
    final = final_gain is not None
    w_up_t = w_up.reshape(d, 2, nf, tf).transpose(1, 2, 0, 3)
    conv_w_t = conv_w.reshape(CONV_WIDTH, 2, nf, tf).transpose(1, 2, 0, 3)
    conv_b_t = conv_b.reshape(2, nf, 1, tf)
    w_down_t = w_down.reshape(nf, tf, d)
    in_specs = [
        pl.BlockSpec((1, tm, d), lambda bi, i: (bi, i, 0)),
        pl.BlockSpec((1, halo, d), lambda bi, i: (bi, jnp.maximum(i * (tm // halo) - 1, 0), 0)),
        _const_spec((1, d)),
        _const_spec(w_up_t.shape),
        _const_spec(conv_w_t.shape),
        _const_spec(conv_b_t.shape),
        _const_spec(w_down_t.shape),
    ]
    args = [x3, x3, g.reshape(1, d), w_up_t, conv_w_t, conv_b_t, w_down_t]
    if final:
        in_specs.append(_const_spec((1, d)))
        args.append(final_gain.reshape(1, d))
    return pl.pallas_call(
        functools.partial(_ffn_kernel, tm=tm, final=final),
        out_shape=jax.ShapeDtypeStruct((b, s, d), F32),
        grid=(b, s // tm),
        in_specs=in_specs,
        out_specs=pl.BlockSpec((1, tm, d), lambda bi, i: (bi, i, 0)),
        scratch_shapes=[
            pltpu.VMEM((tm + halo, d), BF16),
            pltpu.VMEM((2, 2, tm + halo, tf), F32),
        ],
        compiler_params=_params("parallel", "parallel"),
    )(*args)


def _nsa_front_kernel(x_ref, gkv_ref, gq_ref, wkv_ref, wq_ref, wg_ref, cos_ref, sin_ref,
                      kv_ref, q_ref, qr_ref, gt_ref, *, q_scale):
    x = x_ref[...]
    y = x * lax.rsqrt(jnp.mean(x * x, axis=-1, keepdims=True) + EPS)
    hn_kv = (y * gkv_ref[...]).astype(BF16)
    hn_q = (y * gq_ref[...]).astype(BF16)
    cos, sin = cos_ref[...], sin_ref[...]
    rope_w = cos.shape[1]

    gt_ref[0] = jax.nn.sigmoid(_dot(hn_q, wg_ref[...])).T
    for grp in range(wq_ref.shape[1] // rope_w):
        acc = _dot(hn_q, wq_ref[:, grp * rope_w:(grp + 1) * rope_w]) * q_scale
        q_ref[0, grp * rope_w:(grp + 1) * rope_w, :] = acc.T.astype(q_ref.dtype)
        qr_ref[0, grp * rope_w:(grp + 1) * rope_w, :] = _rope_tile(acc, cos, sin).T.astype(qr_ref.dtype)
    for grp in (2, 4, 0, 1, 3, 5):
        acc = _dot(hn_kv, wkv_ref[:, grp * rope_w:(grp + 1) * rope_w])
        if grp in (2, 4):
            acc = _rope_tile(acc, cos, sin)
        kv_ref[:, grp * rope_w:(grp + 1) * rope_w] = acc.astype(kv_ref.dtype)


def _nsa_front(x2, g_kv, g_q, w_kv, w_q, w_g, rope, *, seq, q_scale):
    n, d = x2.shape
    tm = PROJ_ROWS
    per_seq = seq // tm
    batch = n // seq
    rope_w = rope[0].shape[1]
    row = lambda m: pl.BlockSpec((tm, m), lambda i: (i, 0))
    col = lambda m: pl.BlockSpec((1, m, tm), lambda i: (i // per_seq, 0, i % per_seq))
    return pl.pallas_call(
        functools.partial(_nsa_front_kernel, q_scale=q_scale),
        out_shape=[
            jax.ShapeDtypeStruct((n, w_kv.shape[1]), BF16),
            jax.ShapeDtypeStruct((batch, w_q.shape[1], seq), BF16),
            jax.ShapeDtypeStruct((batch, w_q.shape[1], seq), BF16),
            jax.ShapeDtypeStruct((batch, w_g.shape[1], seq), F32),
        ],
        grid=(n // tm,),
        in_specs=[
            row(d), _const_spec((1, d)), _const_spec((1, d)),
            _const_spec(w_kv.shape), _const_spec(w_q.shape), _const_spec(w_g.shape),
            pl.BlockSpec((tm, rope_w), lambda i: (i % per_seq, 0)),
            pl.BlockSpec((tm, rope_w), lambda i: (i % per_seq, 0)),
        ],
        out_specs=[row(w_kv.shape[1]), col(w_q.shape[1]), col(w_q.shape[1]), col(w_g.shape[1])],
        compiler_params=_params("parallel"),
    )(x2, g_kv.reshape(1, d), g_q.reshape(1, d), w_kv, w_q, w_g, *rope)


def _compress_kernel(tk_ref, tv_ref, w1k_ref, w1v_ref, bk_ref, bv_ref, w2k_ref, w2v_ref, o_ref, *, rows_per_seq):
    rows = tk_ref.shape[0]
    row = lax.broadcasted_iota(jnp.int32, (rows, 1), 0)
    valid = jnp.where((row & (rows_per_seq - 1)) < rows_per_seq - 1, 1.0, 0.0)

    def one(t_ref, w1_ref, b_ref, w2_ref):
        y = _dot(t_ref[...], w1_ref[...])
        hid = y.shape[1] // 2
        pre = y[:, :hid] + pltpu.roll(y[:, hid:], rows - 1, 0) + b_ref[0:1, :]
        out = _dot(jax.nn.gelu(pre).astype(BF16), w2_ref[...])
        return out * valid

    o_ref[...] = jnp.concatenate(
        [one(tk_ref, w1k_ref, bk_ref, w2k_ref), one(tv_ref, w1v_ref, bv_ref, w2v_ref)], axis=1
    ).astype(o_ref.dtype)


def _pe_bias_kernel(pk_ref, pv_ref, w1k_ref, w1v_ref, ok_ref, ov_ref):
    ok_ref[...] = _dot(pk_ref[...], w1k_ref[...])
    ov_ref[...] = _dot(pv_ref[...], w1v_ref[...])


def _compress(tk, tv, pe_k, pe_v, w1k, w1v, w2k, w2v, *, rows_per_seq):
    r, kdim = tk.shape
    hid = w1k.shape[1]
    dh = w2k.shape[1]
    tr = CMP_ROWS
    sub = 8
    pk = jnp.broadcast_to(pe_k.reshape(1, -1), (sub, 2 * kdim)).astype(BF16)
    pv = jnp.broadcast_to(pe_v.reshape(1, -1), (sub, 2 * kdim)).astype(BF16)
    bk, bv = pl.pallas_call(
        _pe_bias_kernel,
        out_shape=[jax.ShapeDtypeStruct((sub, hid), F32)] * 2,
    )(pk, pv, w1k, w1v)
    w1k_cat = jnp.concatenate([w1k[:kdim], w1k[kdim:]], axis=1)
    w1v_cat = jnp.concatenate([w1v[:kdim], w1v[kdim:]], axis=1)
    return pl.pallas_call(
        functools.partial(_compress_kernel, rows_per_seq=rows_per_seq),
        out_shape=jax.ShapeDtypeStruct((r, 2 * dh), BF16),
        grid=(r // tr,),
        in_specs=[
            pl.BlockSpec((tr, kdim), lambda i: (i, 0)),
            pl.BlockSpec((tr, kdim), lambda i: (i, 0)),
            _const_spec((kdim, 2 * hid)),
            _const_spec((kdim, 2 * hid)),
            _const_spec((sub, hid)),
            _const_spec((sub, hid)),
            _const_spec((hid, dh)),
            _const_spec((hid, dh)),
        ],
        out_specs=pl.BlockSpec((tr, 2 * dh), lambda i: (i, 0)),
        compiler_params=_params("parallel"),
    )(tk, tv, w1k_cat, w1v_cat, bk, bv, w2k, w2v)


def _cmp_to_sel_weights_t(n_cmp_pad, n_sel):
    c0 = np.arange(n_cmp_pad)[None, :] * CMP_STRIDE
    s0 = np.arange(n_sel)[:, None] * SEL_BLOCK
    ov = np.clip(np.minimum(c0 + CMP_BLOCK, s0 + SEL_BLOCK) - np.maximum(c0, s0), 0, None)
    return (ov / CMP_BLOCK).astype(np.float32)


def _nsa_kernel(q_ref, qr_ref, gt_ref, kc_ref, ks_ref, vs_ref, kw_ref, vw_ref, wt_ref, oh_ref, o_ref,
                ksel_s, kwin_s, vsel_s, vwin_s, vcmp_s, m_s, acc_s, *, tq, tk):
    grp, dh = NSA_GROUP, NSA_DH
    heads = 2 * grp
    qi = pl.program_id(2)
    t0 = qi * tq
    n_sel = wt_ref.shape[0]
    n_cmp_pad = wt_ref.shape[1]
    seq = ks_ref.shape[1]

    @pl.when(qi == 0)
    def _():
        ones_row = jnp.where(lax.broadcasted_iota(jnp.int32, (BF16_SUBLANES, tk), 0) == 0, 1.0, 0.0).astype(BF16)
        for h2 in range(2):
            cols = slice(h2 * dh, (h2 + 1) * dh)
            ksel_s[h2, :, 0:dh] = ks_ref[0, :, cols]
            ksel_s[h2, :, dh:2 * dh] = oh_ref[...]
            kwin_s[h2, :, 0:dh] = kw_ref[0, :, cols]
            kwin_s[h2, :, dh:2 * dh] = jnp.zeros((seq, dh), BF16)
            vcmp_s[h2] = kc_ref[0, h2, :, dh:2 * dh].astype(F32).T.astype(BF16)

        def build(j, carry):
            r0 = pl.multiple_of(j * tk, tk)
            for src, dst in ((vs_ref, vsel_s), (vw_ref, vwin_s)):
                blk_t = src[0, pl.ds(r0, tk), :].astype(F32).T
                for h2 in range(2):
                    dst[h2, j, 0:dh, :] = blk_t[h2 * dh:(h2 + 1) * dh].astype(BF16)
                    dst[h2, j, dh:, :] = ones_row
            return carry

        lax.fori_loop(0, seq // tk, build, 0)

    tpos = t0 + lax.broadcasted_iota(jnp.int32, (1, tq), 1)
    key_iota = lax.broadcasted_iota(jnp.int32, (tk, 1), 0)
    cmp_end = lax.broadcasted_iota(jnp.int32, (n_cmp_pad, 1), 0) * CMP_STRIDE + (CMP_BLOCK - 1)
    cmp_mask = cmp_end <= tpos
    blk = lax.broadcasted_iota(jnp.int32, (n_sel, 1), 0)
    cur = tpos >> 6
    forced = (blk == 0) | (blk == cur) | (blk == cur - 1)
    win_lo = jnp.maximum(tpos - WINDOW, -1)

    def run_tiles(tiles, q_augs):
        chunks = []
        for sub in range(tq // tk):
            for branch, k_s, v_s, h2, tile, mode in tiles:
                kt = jnp.maximum(tile, 0) * (tq // tk) + sub
                k0 = pl.multiple_of(kt * tk, tk)
                shared = dict(k=k_s[h2, pl.ds(k0, tk), :], v=v_s[h2, kt], mode=mode,
                              kpos=tile * tq + sub * tk + key_iota)
                for g in range(grp):
                    chunks.append(dict(shared, slot=(h2 * 2 + branch) * grp + g, q=q_augs[h2][g]))

        def scores(c):
            s = _dot(c["k"], c["q"])
            if c["mode"] == "causal":
                s = jnp.where(c["kpos"] <= tpos, s, NEG_INF)
            elif c["mode"] == "window_start":
                s = jnp.where(c["kpos"] > win_lo, s, NEG_INF)
            elif c["mode"] == "in_sequence":
                s = jnp.where(c["kpos"] >= 0, s, NEG_INF)
            c["s"] = s

        def softmax_update(c):
            m = m_s[c["slot"]]
            m_new = jnp.maximum(m, jnp.max(c["s"], axis=0, keepdims=True))
            c["p"] = jnp.exp2(c.pop("s") - m_new).astype(BF16)
            c["alpha"] = jnp.exp2(m - m_new)
            m_s[c["slot"]] = m_new

        def accumulate(c):
            acc_s[c["slot"]] = c["alpha"] * acc_s[c["slot"]] + _dot(c["v"], c.pop("p"))

        ahead = 6
        for c in chunks[:ahead]:
            scores(c)
        for i, c in enumerate(chunks):
            softmax_update(c)
            if i + ahead < len(chunks):
                scores(chunks[i + ahead])
            accumulate(c)

    def result(branch, h2, g):
        slot = (h2 * 2 + branch) * grp + g
        return acc_s[slot, 0:dh, :] / acc_s[slot, dh:dh + 1, :]

    m_s[...] = jnp.full(m_s.shape, NEG_INF, F32)
    acc_s[...] = jnp.zeros(acc_s.shape, F32)
    q_augs, o_cmps = [], []
    cmp_scores = [[_dot(kc_ref[0, h2, :, 0:dh], q_ref[0, (h2 * grp + g) * dh:(h2 * grp + g + 1) * dh, :])
                   for g in range(grp)] for h2 in range(2)]

    for h2 in range(2):
        head_rows = lambda ref, g: ref[0, (h2 * grp + g) * dh:(h2 * grp + g + 1) * dh, :]

        p_grp = jnp.zeros((n_cmp_pad, tq), F32)
        o_cmp = []
        for g in range(grp):
            s = jnp.where(cmp_mask, cmp_scores[h2][g], NEG_INF)
            e = jnp.where(cmp_mask, jnp.exp2(s - jnp.max(s, axis=0, keepdims=True)), 0.0)
            inv = 1.0 / jnp.maximum(jnp.sum(e, axis=0, keepdims=True), 1e-30)
            o_cmp.append(_dot(vcmp_s[h2], e.astype(BF16)) * inv)
            p_grp = p_grp + e * inv
        i3 = _dot(wt_ref[...], jnp.concatenate(_split3(p_grp), axis=1))
        imp = i3[:, 0:tq] + i3[:, tq:2 * tq] + i3[:, 2 * tq:]
        imp = jnp.where(blk > cur, -1.0, jnp.where(forced, FORCE_SCORE, imp))
        rows = 8
        ranks = []
        for r0 in range(0, n_sel, rows):
            mine = imp[r0:r0 + rows]
            row_id = blk[r0:r0 + rows]
            rank = jnp.zeros((rows, tq), F32)
            for jp in range(n_sel):
                other = imp[jp:jp + 1, :]
                ge = lambda: jnp.where(other >= mine, 1.0, 0.0)
                gt_ = lambda: jnp.where(other > mine, 1.0, 0.0)
                if jp < r0:
                    rank = rank + ge()
                elif jp >= r0 + rows:
                    rank = rank + gt_()
                else:
                    rank = rank + jnp.where(row_id > jp, ge(), gt_())
            ranks.append(rank)
        rank = jnp.concatenate(ranks, axis=0)
        sel_bias = jnp.where(rank < float(SEL_TOPK), 0.0, NEG_INF).astype(BF16)
        pad = jnp.zeros((dh - n_sel, tq), BF16)
        q_augs.append([jnp.concatenate([head_rows(qr_ref, g), sel_bias, pad], axis=0) for g in range(grp)])
        o_cmps.append(o_cmp)

    sel, win = 0, 1

    def sel_tiles(kts):
        return [(sel, ksel_s, vsel_s, h2, kt, None) for kt in kts for h2 in range(2)]

    def sel_body(pair, carry):
        run_tiles(sel_tiles([2 * pair, 2 * pair + 1]), q_augs)
        return carry

    lax.fori_loop(0, qi // 2, sel_body, 0)

    @pl.when(qi % 2 == 1)
    def _():
        run_tiles(sel_tiles([qi - 1]), q_augs)

    run_tiles([(sel, ksel_s, vsel_s, h2, qi, "causal") for h2 in range(2)]
              + [(win, kwin_s, vwin_s, h2, qi - 2, "window_start") for h2 in range(2)]
              + [(win, kwin_s, vwin_s, h2, qi - 1, "in_sequence") for h2 in range(2)]
              + [(win, kwin_s, vwin_s, h2, qi, "causal") for h2 in range(2)], q_augs)

    for h2 in range(2):
        for g in range(grp):
            hg = h2 * grp + g
            gate = lambda branch: gt_ref[0, branch * heads + hg:branch * heads + hg + 1, :]
            o_ref[0, hg * dh:(hg + 1) * dh, :] = (
                gate(0) * o_cmps[h2][g] + gate(1) * result(sel, h2, g) + gate(2) * result(win, h2, g)
            ).astype(o_ref.dtype)


def _nsa_attention(q_t, qr_t, gates_t, kvc, kv):
    b, d, s = q_t.shape
    tq, tk = NSA_TQ, NSA_TK
    assert WINDOW == 2 * tq and tq % SEL_BLOCK == 0 and tq % tk == 0
    dh = NSA_DH
    pairs = NSA_KV_HEADS // 2
    heads = 2 * NSA_GROUP
    pw = heads * dh
    n_cmp_pad = kvc.shape[2]
    n_sel = s // SEL_BLOCK
    wt = jnp.asarray(_cmp_to_sel_weights_t(n_cmp_pad, n_sel), BF16)
    onehot = np.zeros((s, dh), np.float32)
    onehot[np.arange(s), np.arange(s) // SEL_BLOCK] = 1.0
    kv_spec = lambda c6: pl.BlockSpec((1, s, LANE), lambda bi, p, i: (bi, 0, 2 * c6 + p))
    return pl.pallas_call(
        functools.partial(_nsa_kernel, tq=tq, tk=tk),
        out_shape=jax.ShapeDtypeStruct((b, d, s), BF16),
        grid=(b, pairs, s // tq),
        in_specs=[
            pl.BlockSpec((1, pw, tq), lambda bi, p, i: (bi, p, i)),
            pl.BlockSpec((1, pw, tq), lambda bi, p, i: (bi, p, i)),
            pl.BlockSpec((1, LANE, tq), lambda bi, p, i: (bi, p, i)),
            pl.BlockSpec((1, 2, n_cmp_pad, LANE), lambda bi, p, i: (bi, p, 0, 0)),
            kv_spec(2), kv_spec(3), kv_spec(4), kv_spec(5),
            _const_spec(wt.shape),
            _const_spec(onehot.shape),
        ],
        out_specs=pl.BlockSpec((1, pw, tq), lambda bi, p, i: (bi, p, i)),
        scratch_shapes=[
            pltpu.VMEM((2, s, 2 * dh), BF16),
            pltpu.VMEM((2, s, 2 * dh), BF16),
            pltpu.VMEM((2, s // tk, NSA_ACC_ROWS, tk), BF16),
            pltpu.VMEM((2, s // tk, NSA_ACC_ROWS, tk), BF16),
            pltpu.VMEM((2, dh, n_cmp_pad), BF16),
            pltpu.VMEM((2 * heads, 1, tq), F32),
            pltpu.VMEM((2 * heads, NSA_ACC_ROWS, tq), F32),
        ],
        compiler_params=_params("parallel", "parallel", "arbitrary"),
    )(q_t, qr_t, gates_t, kvc, kv, kv, kv, kv, wt, jnp.asarray(onehot, BF16))


def _rope_tables(seq, width):
    half = NSA_DH // 2
    inv = ROPE_THETA ** (-jnp.arange(half, dtype=F32) / half)
    ang = jnp.arange(seq, dtype=F32)[:, None] * inv[None, :]
    cos, sin = jnp.cos(ang), jnp.sin(ang)
    reps = width // NSA_DH
    cos_full = jnp.tile(jnp.concatenate([cos, cos], axis=1), (1, reps))
    sin_signed = jnp.tile(jnp.concatenate([-sin, sin], axis=1), (1, reps))
    return cos_full, sin_signed


def _nsa_gate_weights(w_gate):
    d = w_gate.shape[0]
    pairs = NSA_KV_HEADS // 2
    heads = 2 * NSA_GROUP
    wg = w_gate.reshape(d, 3, pairs, heads).transpose(0, 2, 1, 3).reshape(d, pairs, 3 * heads)
    return jnp.pad(wg, ((0, 0), (0, 0), (0, LANE - 3 * heads))).reshape(d, pairs * LANE)


def kernel(x, norm_mix, norm_ffn, gla_w_in, gla_w_alpha_up, gla_b_alpha, gla_norm, gla_w_o, kv_norm, nsa_w_kv, cmp_pe_k, cmp_pe_v, cmp_k_w1, cmp_k_w2, cmp_v_w1, cmp_v_w2, nsa_w_in, nsa_w_o, ffn_w_up, ffn_conv_w, ffn_conv_b, ffn_w_down, norm_final):
    b, s, d = x.shape
    n = b * s
    x2 = x.reshape(n, d)

    n_main = 2 * GLA_HEADS * GLA_DK + 2 * GLA_HEADS * GLA_DV
    w_in = gla_w_in[0]
    n_pairs = len(GATE_PIECE_PAIRS)
    lane_pad = LANE - n_pairs * GLA_GATE_RANK
    w_low = jnp.pad(jnp.tile(w_in[:, n_main:], (1, n_pairs)), ((0, 0), (0, lane_pad))).astype(BF16)
    w_up_pieces = _split3(gla_w_alpha_up[0])
    w_up6 = jnp.pad(jnp.concatenate([w_up_pieces[iw] for _, iw in GATE_PIECE_PAIRS], axis=0),
                    ((0, lane_pad), (0, 0)))
    proj, log_a = _gla_front(x2, norm_mix[0], w_in[:, :n_main].astype(BF16), w_low, w_up6, gla_b_alpha[0])
    o = _gla_core(proj.reshape(b, s, n_main), log_a.reshape(b, s, -1), gla_norm[0])
    x2 = _proj_res(o.reshape(n, -1), gla_w_o[0].astype(BF16), x2)
    x2 = _conv_ffn(x2.reshape(b, s, d), norm_ffn[0], ffn_w_up[0].astype(BF16), ffn_conv_w[0], ffn_conv_b[0],
                   ffn_w_down[0].astype(BF16)).reshape(n, d)

    hk, dh = NSA_KV_HEADS, NSA_DH
    nq = NSA_HEADS * dh
    w_nsa = nsa_w_in[0]
    kv, q_t, qr_t, gates_t = _nsa_front(
        x2, kv_norm, norm_mix[1], nsa_w_kv.astype(BF16), w_nsa[:, :nq].astype(BF16),
        _nsa_gate_weights(w_nsa[:, nq:]).astype(BF16), _rope_tables(s, 4 * dh), seq=s, q_scale=dh ** -0.5 * LOG2E)
    kv4 = kv.reshape(b, s, 6, hk, dh)
    rows_per_seq = s // CMP_STRIDE

    def token_groups(t):
        return t.transpose(0, 2, 1, 3).reshape(b * hk * rows_per_seq, CMP_STRIDE * dh)

    kvc = _compress(token_groups(kv4[:, :, 0]), token_groups(kv4[:, :, 1]), cmp_pe_k, cmp_pe_v,
                    cmp_k_w1.astype(BF16), cmp_v_w1.astype(BF16), cmp_k_w2.astype(BF16),
                    cmp_v_w2.astype(BF16), rows_per_seq=rows_per_seq)
    kvc = kvc.reshape(b, hk, rows_per_seq, 2 * dh)

    o_t = _nsa_attention(q_t, qr_t, gates_t, kvc, kv.reshape(b, s, -1))
    x2 = _proj_res(o_t, nsa_w_o[0].astype(BF16), x2, transposed=True)
    return _conv_ffn(x2.reshape(b, s, d), norm_ffn[1], ffn_w_up[1].astype(BF16), ffn_conv_w[1], ffn_conv_b[1],
                     ffn_w_down[1].astype(BF16), norm_final)
```

```python
import functools
import math

import numpy as np
import jax
import jax.numpy as jnp
from jax import lax
from jax.experimental import pallas as pl
from jax.experimental.pallas import tpu as pltpu

F32 = jnp.float32
BF16 = jnp.bfloat16

EPS = 1e-6
NEG_INF = -1e30
ROPE_THETA = 10000.0
LOG2E = math.log2(math.e)

LANE = 128
BF16_SUBLANES = 16
VMEM_LIMIT_BYTES = 56 * 1024 * 1024

GLA_HEADS = 4
GLA_DK = 128
GLA_DV = 256
GLA_GATE_RANK = 16
GLA_TAU = 16.0
GLA_CHUNK = 64
GLA_LEVEL_GROUP = 4
GATE_PIECE_PAIRS = ((0, 0), (0, 1), (1, 0), (0, 2), (1, 1), (2, 0))
NSA_HEADS = 16
NSA_KV_HEADS = 4
NSA_GROUP = 4
NSA_DH = 64
CMP_BLOCK = 32
CMP_STRIDE = 16
CMP_HIDDEN = 256
SEL_BLOCK = 64
SEL_TOPK = 16
WINDOW = 512
FORCE_SCORE = 1e4
CONV_WIDTH = 3
CONV_HALO = 16

PROJ_ROWS = 1024
PROJ_COLS = 512
RES_ROWS = 1024
GLA_ROWS = 1024
GLA_CHUNKS_PER_TRIP = 4
FFN_ROWS = 1024
FFN_COLS = 256
FFN_TILES_PER_TRIP = 4
CMP_ROWS = 512
NSA_TQ = 256
NSA_TK = 256
NSA_ACC_ROWS = NSA_DH + BF16_SUBLANES


def _dot(a, b):
    return jnp.dot(a, b, preferred_element_type=F32)


def _dot_nt(a, b):
    return lax.dot_general(a, b, (((1,), (1,)), ((), ())), preferred_element_type=F32)


def _dot_tn(a, b):
    return lax.dot_general(a, b, (((0,), (0,)), ((), ())), preferred_element_type=F32)


def _rms(x, g):
    return x * lax.rsqrt(jnp.mean(x * x, axis=-1, keepdims=True) + EPS) * g


def _split3(x):
    hi = x.astype(BF16)
    r1 = x - hi.astype(F32)
    mid = r1.astype(BF16)
    lo = (r1 - mid.astype(F32)).astype(BF16)
    return hi, mid, lo


def _rope_tile(x, cos, sin_signed):
    w = x.shape[-1]
    lane = lax.broadcasted_iota(jnp.int32, x.shape, 1)
    first_half = (lane & (NSA_DH - 1)) < (NSA_DH // 2)
    partner = jnp.where(first_half, pltpu.roll(x, w - NSA_DH // 2, 1), pltpu.roll(x, NSA_DH // 2, 1))
    return x * cos + partner * sin_signed


def _params(*sem):
    return pltpu.CompilerParams(dimension_semantics=sem, vmem_limit_bytes=VMEM_LIMIT_BYTES)


def _const_spec(shape):
    nd = len(shape)
    return pl.BlockSpec(shape, lambda *_: (0,) * nd, pipeline_mode=pl.Buffered(1))


def _gla_front_kernel(x_ref, g_ref, w_ref, wl_ref, wu_ref, b_ref, o_ref, la_ref):
    hn = _rms(x_ref[...], g_ref[...]).astype(BF16)
    a_rep = _dot(hn, wl_ref[...])
    hi, mid, lo = _split3(a_rep)
    group = lax.broadcasted_iota(jnp.int32, a_rep.shape, 1) // GLA_GATE_RANK
    a_piece = [g for g, (ia, _) in enumerate(GATE_PIECE_PAIRS)]
    is_piece = lambda p: functools.reduce(
        jnp.logical_or, [group == g for g in a_piece if GATE_PIECE_PAIRS[g][0] == p])
    a6 = jnp.where(is_piece(2), lo, jnp.where(is_piece(1), mid, hi))
    z = b_ref[...] + _dot(a6, wu_ref[...])
    log_sig = jnp.minimum(z, 0.0) - jnp.log(1.0 + jnp.exp(-jnp.abs(z)))
    la_ref[...] = log_sig * (1.0 / GLA_TAU)
    m = w_ref.shape[1]
    for c in range(0, m, PROJ_COLS):
        o_ref[:, c:c + PROJ_COLS] = _dot(hn, w_ref[:, c:c + PROJ_COLS]).astype(o_ref.dtype)


def _gla_front(x2, g, w_main, w_low, w_up6, b):
    n, d = x2.shape
    m = w_main.shape[1]
    hk = w_up6.shape[-1]
    tm = PROJ_ROWS
    return pl.pallas_call(
        _gla_front_kernel,
        out_shape=[jax.ShapeDtypeStruct((n, m), BF16), jax.ShapeDtypeStruct((n, hk), F32)],
        grid=(n // tm,),
        in_specs=[
            pl.BlockSpec((tm, d), lambda i: (i, 0)),
            _const_spec((1, d)),
            _const_spec((d, m)),
            _const_spec((d, LANE)),
            _const_spec((LANE, hk)),
            _const_spec((1, hk)),
        ],
        out_specs=[pl.BlockSpec((tm, m), lambda i: (i, 0)), pl.BlockSpec((tm, hk), lambda i: (i, 0))],
        compiler_params=_params("parallel"),
    )(x2, g.reshape(1, d), w_main, w_low, w_up6, b.reshape(1, hk))


def _gla_constants():
    c = GLA_CHUNK
    idx = np.arange(c)
    row, col = idx[:, None], idx[None, :]
    mats = [col <= row, col > row]
    masks = []
    h = 1
    while h < c:
        blk = idx // (2 * h)
        upper = (idx % (2 * h)) >= h
        r = (blk * 2 * h + h - 1)[:, None]
        mats.append(np.where(upper[:, None], (col > r) & (col <= row), (col > row) & (col <= r)))
        masks.append((blk[:, None] == blk[None, :]) & upper[:, None] & (~upper[None, :]))
        h *= 2
    masks.append(np.eye(c, dtype=bool))
    strips = np.zeros((len(masks), c, LANE), np.float32)
    for lvl, m in enumerate(masks):
        half = (lvl % GLA_LEVEL_GROUP) % (LANE // c)
        strips[lvl, :, half * c:(half + 1) * c] = m
    return np.concatenate(mats, 0).astype(np.float32), strips


def _gla_kernel(q_ref, k_ref, v_ref, r_ref, g_ref, gs_ref, mk_ref, gn_ref, o_ref, st_ref, *, n_chunks):
    c, dk, dv = GLA_CHUNK, GLA_DK, GLA_DV
    n_levels = mk_ref.shape[0]
    per_strip = LANE // c
    q_scale = dk ** -0.5

    @pl.when(pl.program_id(1) == 0)
    def _():
        st_ref[...] = jnp.zeros_like(st_ref)

    def decays(h, r0, t):
        ks = slice(h * dk, (h + 1) * dk)
        t["q"] = q_ref[0, pl.ds(r0, c), ks].astype(F32) * q_scale
        t["k"] = k_ref[0, pl.ds(r0, c), ks].astype(F32)
        t["v"] = v_ref[0, pl.ds(r0, c), h * dv:(h + 1) * dv]
        hi, mid, _ = _split3(g_ref[0, pl.ds(r0, c), ks])
        t["e2"] = _dot(gs_ref[...], jnp.concatenate([hi, mid], axis=1))

    def level_products(t):
        e2 = t.pop("e2")
        x = t["x"] = jnp.exp(e2[:, :dk] + e2[:, dk:])
        q, k = t["q"], t["k"]
        t["prods"] = []
        for first in range(0, n_levels, GLA_LEVEL_GROUP):
            lvls = list(range(first, min(first + GLA_LEVEL_GROUP, n_levels)))
            factor = lambda lvl: x[(2 + lvl) * c:(3 + lvl) * c] if lvl < n_levels - 1 else 1.0
            q_stack = jnp.concatenate([(q * factor(lvl)).astype(BF16) for lvl in lvls], axis=0)
            k_stack = jnp.concatenate([(k * factor(lvl)).astype(BF16) for lvl in lvls]
                                      + [jnp.zeros((c, dk), BF16)] * (GLA_LEVEL_GROUP - len(lvls)), axis=0)
            t["prods"].append((lvls, _dot_nt(q_stack, k_stack)))
        t["q_inter"] = (q * x[0:c]).astype(BF16)
        t["k_state"] = (k * x[c:2 * c]).astype(BF16)

    def outputs(h, t):
        strip = jnp.zeros((c, LANE), F32)
        for lvls, prod in t.pop("prods"):
            for j, lvl in enumerate(lvls):
                col = (j // per_strip) * LANE
                strip = strip + prod[j * c:(j + 1) * c, col:col + LANE] * mk_ref[lvl]
        att = strip[:, 0:c]
        for part in range(1, per_strip):
            att = att + strip[:, part * c:(part + 1) * c]
        st = st_ref[h]
        t["o"] = _dot(att.astype(BF16), t["v"]) + _dot_nt(t["q_inter"], st.astype(BF16))
        st_ref[h] = st * t["x"][c - 1:c] + _dot_tn(t["v"], t["k_state"])

    def finish(h, r0, t):
        o = t["o"]
        o = o * lax.rsqrt(jnp.mean(o * o, axis=-1, keepdims=True) + EPS) * gn_ref[h:h + 1, :]
        r = r_ref[0, pl.ds(r0, c), h * dv:(h + 1) * dv].astype(F32)
        o_ref[0, pl.ds(r0, c), h * dv:(h + 1) * dv] = (o * (r * jax.nn.sigmoid(r))).astype(o_ref.dtype)

    def body(ci, carry):
        items = [(h, pl.multiple_of((ci * GLA_CHUNKS_PER_TRIP + sub) * c, c), dict())
                 for sub in range(GLA_CHUNKS_PER_TRIP) for h in range(GLA_HEADS)]
        for h, r0, t in items:
            decays(h, r0, t)
        for h, r0, t in items:
            level_products(t)
        for h, r0, t in items:
            outputs(h, t)
        for h, r0, t in items:
            finish(h, r0, t)
        return carry

    lax.fori_loop(0, n_chunks // GLA_CHUNKS_PER_TRIP, body, 0)


def _gla_core(proj, log_a, gla_norm):
    b, s, _ = proj.shape
    h, dk, dv = GLA_HEADS, GLA_DK, GLA_DV
    ts = GLA_ROWS
    gs, mk = _gla_constants()
    return pl.pallas_call(
        functools.partial(_gla_kernel, n_chunks=ts // GLA_CHUNK),
        out_shape=jax.ShapeDtypeStruct((b, s, h * dv), BF16),
        grid=(b, s // ts),
        in_specs=[
            pl.BlockSpec((1, ts, h * dk), lambda i, j: (i, j, 0)),
            pl.BlockSpec((1, ts, h * dk), lambda i, j: (i, j, 1)),
            pl.BlockSpec((1, ts, h * dv), lambda i, j: (i, j, 1)),
            pl.BlockSpec((1, ts, h * dv), lambda i, j: (i, j, 2)),
            pl.BlockSpec((1, ts, h * dk), lambda i, j: (i, j, 0)),
            _const_spec(gs.shape),
            _const_spec(mk.shape),
            _const_spec((h, dv)),
        ],
        out_specs=pl.BlockSpec((1, ts, h * dv), lambda i, j: (i, j, 0)),
        scratch_shapes=[pltpu.VMEM((h, dv, dk), F32)],
        compiler_params=_params("parallel", "arbitrary"),
    )(proj, proj, proj, proj, log_a, jnp.asarray(gs, BF16), jnp.asarray(mk), gla_norm)


def _proj_res_kernel(a_ref, w_ref, r_ref, o_ref):
    o_ref[...] = r_ref[...] + _dot(a_ref[...], w_ref[...])


def _proj_res_t_kernel(a_ref, w_ref, r_ref, o_ref):
    o_ref[...] = r_ref[...] + _dot_tn(a_ref[0], w_ref[...])


def _proj_res(a, w, res, *, transposed=False):
    n, d = res.shape
    k = w.shape[0]
    tm = RES_ROWS
    if transposed:
        per_seq = a.shape[2] // tm
        a_spec = pl.BlockSpec((1, k, tm), lambda i: (i // per_seq, 0, i % per_seq))
    else:
        a_spec = pl.BlockSpec((tm, k), lambda i: (i, 0))
    return pl.pallas_call(
        _proj_res_t_kernel if transposed else _proj_res_kernel,
        out_shape=jax.ShapeDtypeStruct((n, d), F32),
        grid=(n // tm,),
        in_specs=[
            a_spec,
            _const_spec((k, d)),
            pl.BlockSpec((tm, d), lambda i: (i, 0)),
        ],
        out_specs=pl.BlockSpec((tm, d), lambda i: (i, 0)),
        compiler_params=_params("parallel"),
    )(a, w, res)


def _ffn_kernel(*refs, tm, final):
    if final:
        xm_ref, xh_ref, gn_ref, wup_ref, cw_ref, cb_ref, wd_ref, gf_ref, o_ref, hn_ref, u_ref = refs
    else:
        xm_ref, xh_ref, gn_ref, wup_ref, cw_ref, cb_ref, wd_ref, o_ref, hn_ref, u_ref = refs
    halo = CONV_HALO
    nf = wd_ref.shape[0]
    g = gn_ref[...]
    hn_ref[0:halo, :] = _rms(xh_ref[0], g).astype(BF16)
    hn_ref[halo:, :] = _rms(xm_ref[0], g).astype(BF16)
    o_ref[0] = xm_ref[0]
    keep = jnp.where(pl.program_id(1) > 0, 1.0, 0.0)

    def up_project(fi, slot):
        hn = hn_ref[...]
        for gv in range(2):
            u = _dot(hn, wup_ref[gv, fi])
            u_ref[slot, gv] = u
            u_ref[slot, gv, 0:halo, :] = u[0:halo] * keep

    def activation(fi, slot):
        outs = []
        for gv in range(2):
            out = cb_ref[gv, fi]
            for tap in range(CONV_WIDTH):
                start = halo - (CONV_WIDTH - 1) + tap
                out = out + u_ref[slot, gv, start:start + tm, :] * cw_ref[gv, fi, tap:tap + 1, :]
            outs.append(out)
        gate, val = outs
        return (gate * jax.nn.sigmoid(gate) * val).astype(BF16)

    def run(tiles):
        up_project(tiles[0], 0)
        for j, fi in enumerate(tiles):
            if j + 1 < len(tiles):
                up_project(tiles[j + 1], (j + 1) % 2)
            o_ref[0] += _dot(activation(fi, j % 2), wd_ref[fi])

    group = FFN_TILES_PER_TRIP

    def trip(ti, carry):
        run([ti * group + j for j in range(group)])
        return carry

    lax.fori_loop(0, nf // group, trip, 0)
    if nf % group:
        run(list(range(nf - nf % group, nf)))
    if final:
        o_ref[0] = _rms(o_ref[0], gf_ref[...])


def _conv_ffn(x3, g, w_up, conv_w, conv_b, w_down, final_gain=None):
    b, s, d = x3.shape
    ffn = w_down.shape[0]
    tm, tf = FFN_ROWS, FFN_COLS
    nf = ffn // tf
    halo = CONV_HALO
    final = final_gain is not None
    w_up_t = w_up.reshape(d, 2, nf, tf).transpose(1, 2, 0, 3)
    conv_w_t = conv_w.reshape(CONV_WIDTH, 2, nf, tf).transpose(1, 2, 0, 3)
    conv_b_t = conv_b.reshape(2, nf, 1, tf)
    w_down_t = w_down.reshape(nf, tf, d)
    in_specs = [
        pl.BlockSpec((1, tm, d), lambda bi, i: (bi, i, 0)),
        pl.BlockSpec((1, halo, d), lambda bi, i: (bi, jnp.maximum(i * (tm // halo) - 1, 0), 0)),
        _const_spec((1, d)),
        _const_spec(w_up_t.shape),
        _const_spec(conv_w_t.shape),
        _const_spec(conv_b_t.shape),
        _const_spec(w_down_t.shape),
    ]
    args = [x3, x3, g.reshape(1, d), w_up_t, conv_w_t, conv_b_t, w_down_t]
    if final:
        in_specs.append(_const_spec((1, d)))
        args.append(final_gain.reshape(1, d))
    return pl.pallas_call(
        functools.partial(_ffn_kernel, tm=tm, final=final),
        out_shape=jax.ShapeDtypeStruct((b, s, d), F32),
        grid=(b, s // tm),
        in_specs=in_specs,
        out_specs=pl.BlockSpec((1, tm, d), lambda bi, i: (bi, i, 0)),
        scratch_shapes=[
            pltpu.VMEM((tm + halo, d), BF16),
            pltpu.VMEM((2, 2, tm + halo, tf), F32),
        ],
        compiler_params=_params("parallel", "parallel"),
    )(*args)


def _nsa_front_kernel(x_ref, gkv_ref, gq_ref, wkv_ref, wq_ref, wg_ref, cos_ref, sin_ref,
                      kv_ref, k16_ref, v16_ref, q_ref, qr_ref, gt_ref, raw_s, *, q_scale):
    x = x_ref[...]
    y = x * lax.rsqrt(jnp.mean(x * x, axis=-1, keepdims=True) + EPS)
    hn_kv = (y * gkv_ref[...]).astype(BF16)
    hn_q = (y * gq_ref[...]).astype(BF16)
    cos, sin = cos_ref[...], sin_ref[...]
    rope_w = cos.shape[1]

    gt_ref[0] = jax.nn.sigmoid(_dot(hn_q, wg_ref[...])).T
    for grp in range(wq_ref.shape[1] // rope_w):
        acc = _dot(hn_q, wq_ref[:, grp * rope_w:(grp + 1) * rope_w]) * q_scale
        q_ref[0, grp * rope_w:(grp + 1) * rope_w, :] = acc.T.astype(q_ref.dtype)
        qr_ref[0, grp * rope_w:(grp + 1) * rope_w, :] = _rope_tile(acc, cos, sin).T.astype(qr_ref.dtype)
    n_raw = 2
    for grp in (2, 4, 3, 5):
        acc = _dot(hn_kv, wkv_ref[:, grp * rope_w:(grp + 1) * rope_w])
        if grp in (2, 4):
            acc = _rope_tile(acc, cos, sin)
        kv_ref[:, (grp - n_raw) * rope_w:(grp - n_raw + 1) * rope_w] = acc.astype(kv_ref.dtype)
    strips, groups = raw_s.shape[0], raw_s.shape[1] // CMP_STRIDE
    for grp, out_ref in ((0, k16_ref), (1, v16_ref)):
        acc = _dot(hn_kv, wkv_ref[:, grp * rope_w:(grp + 1) * rope_w])
        for j in range(strips):
            raw_s[j] = acc[:, j * LANE:(j + 1) * LANE]
        for r in range(CMP_STRIDE):
            for j in range(strips):
                out_ref[:, r * rope_w + j * LANE:r * rope_w + (j + 1) * LANE] = (
                    raw_s[j, pl.ds(r, groups, stride=CMP_STRIDE), :].astype(out_ref.dtype))


def _nsa_front(x2, g_kv, g_q, w_kv, w_q, w_g, rope, *, seq, q_scale):
    n, d = x2.shape
    tm = PROJ_ROWS
    per_seq = seq // tm
    batch = n // seq
    rope_w = rope[0].shape[1]
    row = lambda m: pl.BlockSpec((tm, m), lambda i: (i, 0))
    col = lambda m: pl.BlockSpec((1, m, tm), lambda i: (i // per_seq, 0, i % per_seq))
    n_att = w_kv.shape[1] - 2 * rope_w
    grouped = CMP_STRIDE * rope_w
    return pl.pallas_call(
        functools.partial(_nsa_front_kernel, q_scale=q_scale),
        out_shape=[
            jax.ShapeDtypeStruct((n, n_att), BF16),
            jax.ShapeDtypeStruct((n // CMP_STRIDE, grouped), BF16),
            jax.ShapeDtypeStruct((n // CMP_STRIDE, grouped), BF16),
            jax.ShapeDtypeStruct((batch, w_q.shape[1], seq), BF16),
            jax.ShapeDtypeStruct((batch, w_q.shape[1], seq), BF16),
            jax.ShapeDtypeStruct((batch, w_g.shape[1], seq), F32),
        ],
        grid=(n // tm,),
        in_specs=[
            row(d), _const_spec((1, d)), _const_spec((1, d)),
            _const_spec(w_kv.shape), _const_spec(w_q.shape), _const_spec(w_g.shape),
            pl.BlockSpec((tm, rope_w), lambda i: (i % per_seq, 0)),
            pl.BlockSpec((tm, rope_w), lambda i: (i % per_seq, 0)),
        ],
        out_specs=[row(n_att),
                   pl.BlockSpec((tm // CMP_STRIDE, grouped), lambda i: (i, 0)),
                   pl.BlockSpec((tm // CMP_STRIDE, grouped), lambda i: (i, 0)),
                   col(w_q.shape[1]), col(w_q.shape[1]), col(w_g.shape[1])],
        scratch_shapes=[pltpu.VMEM((rope_w // LANE, tm, LANE), F32)],
        compiler_params=_params("parallel"),
    )(x2, g_kv.reshape(1, d), g_q.reshape(1, d), w_kv, w_q, w_g, *rope)


def _compress_kernel(tk_ref, tv_ref, w1k_ref, w1v_ref, bk_ref, bv_ref, w2k_ref, w2v_ref, o_ref):
    rows = tk_ref.shape[0]
    dh = NSA_DH
    width = NSA_KV_HEADS * dh
    row = lax.broadcasted_iota(jnp.int32, (rows, 1), 0)
    valid = jnp.where(row < rows - 1, 1.0, 0.0)

    def first_layer(t_ref, w1_ref, h):
        y = None
        for r in range(CMP_STRIDE):
            part = _dot(t_ref[:, r * width + h * dh:r * width + (h + 1) * dh], w1_ref[r * dh:(r + 1) * dh, :])
            y = part if y is None else y + part
        return y

    def second_layer(y, b_ref, w2_ref):
        hid = y.shape[1] // 2
        pre = y[:, :hid] + pltpu.roll(y[:, hid:], rows - 1, 0) + b_ref[0:1, :]
        return _dot(jax.nn.gelu(pre).astype(BF16), w2_ref[...]) * valid

    for h in range(NSA_KV_HEADS):
        yk = first_layer(tk_ref, w1k_ref, h)
        yv = first_layer(tv_ref, w1v_ref, h)
        o_ref[0, h] = jnp.concatenate(
            [second_layer(yk, bk_ref, w2k_ref), second_layer(yv, bv_ref, w2v_ref)], axis=1).astype(o_ref.dtype)


def _pe_bias_kernel(pk_ref, pv_ref, w1k_ref, w1v_ref, ok_ref, ov_ref):
    ok_ref[...] = _dot(pk_ref[...], w1k_ref[...])
    ov_ref[...] = _dot(pv_ref[...], w1v_ref[...])


def _compress(tk, tv, pe_k, pe_v, w1k, w1v, w2k, w2v, *, rows_per_seq):
    r, width = tk.shape
    hid = w1k.shape[1]
    dh = w2k.shape[1]
    kdim = CMP_STRIDE * dh
    batch = r // rows_per_seq
    sub = 8
    pk = jnp.broadcast_to(pe_k.reshape(1, -1), (sub, 2 * kdim)).astype(BF16)
    pv = jnp.broadcast_to(pe_v.reshape(1, -1), (sub, 2 * kdim)).astype(BF16)
    bk, bv = pl.pallas_call(
        _pe_bias_kernel,
        out_shape=[jax.ShapeDtypeStruct((sub, hid), F32)] * 2,
    )(pk, pv, w1k, w1v)
    w1k_cat = jnp.concatenate([w1k[:kdim], w1k[kdim:]], axis=1)
    w1v_cat = jnp.concatenate([w1v[:kdim], w1v[kdim:]], axis=1)
    return pl.pallas_call(
        _compress_kernel,
        out_shape=jax.ShapeDtypeStruct((batch, NSA_KV_HEADS, rows_per_seq, 2 * dh), BF16),
        grid=(batch,),
        in_specs=[
            pl.BlockSpec((rows_per_seq, width), lambda i: (i, 0)),
            pl.BlockSpec((rows_per_seq, width), lambda i: (i, 0)),
            _const_spec((kdim, 2 * hid)),
            _const_spec((kdim, 2 * hid)),
            _const_spec((sub, hid)),
            _const_spec((sub, hid)),
            _const_spec((hid, dh)),
            _const_spec((hid, dh)),
        ],
        out_specs=pl.BlockSpec((1, NSA_KV_HEADS, rows_per_seq, 2 * dh), lambda i: (i, 0, 0, 0)),
        compiler_params=_params("parallel"),
    )(tk, tv, w1k_cat, w1v_cat, bk, bv, w2k, w2v)


def _cmp_to_sel_weights_t(n_cmp_pad, n_sel):
    c0 = np.arange(n_cmp_pad)[None, :] * CMP_STRIDE
    s0 = np.arange(n_sel)[:, None] * SEL_BLOCK
    ov = np.clip(np.minimum(c0 + CMP_BLOCK, s0 + SEL_BLOCK) - np.maximum(c0, s0), 0, None)
    return (ov / CMP_BLOCK).astype(np.float32)


def _nsa_kernel(q_ref, qr_ref, gt_ref, kc_ref, ks_ref, vs_ref, kw_ref, vw_ref, wt_ref, oh_ref, o_ref,
                ksel_s, kwin_s, vsel_s, vwin_s, vcmp_s, m_s, acc_s, *, tq, tk):
    grp, dh = NSA_GROUP, NSA_DH
    heads = 2 * grp
    qi = pl.program_id(2)
    t0 = qi * tq
    n_sel = wt_ref.shape[0]
    n_cmp_pad = wt_ref.shape[1]
    seq = ks_ref.shape[1]

    @pl.when(qi == 0)
    def _():
        ones_row = jnp.where(lax.broadcasted_iota(jnp.int32, (BF16_SUBLANES, tk), 0) == 0, 1.0, 0.0).astype(BF16)
        for h2 in range(2):
            cols = slice(h2 * dh, (h2 + 1) * dh)
            ksel_s[h2, :, 0:dh] = ks_ref[0, :, cols]
            ksel_s[h2, :, dh:2 * dh] = oh_ref[...]
            kwin_s[h2, :, 0:dh] = kw_ref[0, :, cols]
            kwin_s[h2, :, dh:2 * dh] = jnp.zeros((seq, dh), BF16)
            vcmp_s[h2] = kc_ref[0, h2, :, dh:2 * dh].astype(F32).T.astype(BF16)

        def build(j, carry):
            r0 = pl.multiple_of(j * tk, tk)
            for src, dst in ((vs_ref, vsel_s), (vw_ref, vwin_s)):
                blk_t = src[0, pl.ds(r0, tk), :].astype(F32).T
                for h2 in range(2):
                    dst[h2, j, 0:dh, :] = blk_t[h2 * dh:(h2 + 1) * dh].astype(BF16)
                    dst[h2, j, dh:, :] = ones_row
            return carry

        lax.fori_loop(0, seq // tk, build, 0)

    tpos = t0 + lax.broadcasted_iota(jnp.int32, (1, tq), 1)
    key_iota = lax.broadcasted_iota(jnp.int32, (tk, 1), 0)
    cmp_end = lax.broadcasted_iota(jnp.int32, (n_cmp_pad, 1), 0) * CMP_STRIDE + (CMP_BLOCK - 1)
    cmp_mask = cmp_end <= tpos
    blk = lax.broadcasted_iota(jnp.int32, (n_sel, 1), 0)
    cur = tpos >> 6
    forced = (blk == 0) | (blk == cur) | (blk == cur - 1)
    win_lo = jnp.maximum(tpos - WINDOW, -1)

    def run_tiles(tiles, q_augs):
        chunks = []
        for sub in range(tq // tk):
            for branch, k_s, v_s, h2, tile, mode in tiles:
                kt = jnp.maximum(tile, 0) * (tq // tk) + sub
                k0 = pl.multiple_of(kt * tk, tk)
                shared = dict(k=k_s[h2, pl.ds(k0, tk), :], v=v_s[h2, kt], mode=mode,
                              kpos=tile * tq + sub * tk + key_iota)
                for g in range(grp):
                    chunks.append(dict(shared, slot=(h2 * 2 + branch) * grp + g, q=q_augs[h2][g]))

        def scores(c):
            s = _dot(c["k"], c["q"])
            if c["mode"] == "causal":
                s = jnp.where(c["kpos"] <= tpos, s, NEG_INF)
            elif c["mode"] == "window_start":
                s = jnp.where(c["kpos"] > win_lo, s, NEG_INF)
            elif c["mode"] == "in_sequence":
                s = jnp.where(c["kpos"] >= 0, s, NEG_INF)
            c["s"] = s

        def softmax_update(c):
            m = m_s[c["slot"]]
            m_new = jnp.maximum(m, jnp.max(c["s"], axis=0, keepdims=True))
            c["p"] = jnp.exp2(c.pop("s") - m_new).astype(BF16)
            c["alpha"] = jnp.exp2(m - m_new)
            m_s[c["slot"]] = m_new

        def accumulate(c):
            acc_s[c["slot"]] = c["alpha"] * acc_s[c["slot"]] + _dot(c["v"], c.pop("p"))

        ahead = 6
        for c in chunks[:ahead]:
            scores(c)
        for i, c in enumerate(chunks):
            softmax_update(c)
            if i + ahead < len(chunks):
                scores(chunks[i + ahead])
            accumulate(c)

    def result(branch, h2, g):
        slot = (h2 * 2 + branch) * grp + g
        return acc_s[slot, 0:dh, :] / acc_s[slot, dh:dh + 1, :]

    m_s[...] = jnp.full(m_s.shape, NEG_INF, F32)
    acc_s[...] = jnp.zeros(acc_s.shape, F32)
    q_augs, o_cmps = [], []
    cmp_scores = [[_dot(kc_ref[0, h2, :, 0:dh], q_ref[0, (h2 * grp + g) * dh:(h2 * grp + g + 1) * dh, :])
                   for g in range(grp)] for h2 in range(2)]

    for h2 in range(2):
        head_rows = lambda ref, g: ref[0, (h2 * grp + g) * dh:(h2 * grp + g + 1) * dh, :]

        p_grp = jnp.zeros((n_cmp_pad, tq), F32)
        o_cmp = []
        for g in range(grp):
            s = jnp.where(cmp_mask, cmp_scores[h2][g], NEG_INF)
            e = jnp.where(cmp_mask, jnp.exp2(s - jnp.max(s, axis=0, keepdims=True)), 0.0)
            inv = 1.0 / jnp.maximum(jnp.sum(e, axis=0, keepdims=True), 1e-30)
            o_cmp.append(_dot(vcmp_s[h2], e.astype(BF16)) * inv)
            p_grp = p_grp + e * inv
        i3 = _dot(wt_ref[...], jnp.concatenate(_split3(p_grp), axis=1))
        imp = i3[:, 0:tq] + i3[:, tq:2 * tq] + i3[:, 2 * tq:]
        imp = jnp.where(blk > cur, -1.0, jnp.where(forced, FORCE_SCORE, imp))
        rows = 8
        ranks = []
        for r0 in range(0, n_sel, rows):
            mine = imp[r0:r0 + rows]
            row_id = blk[r0:r0 + rows]
            rank = jnp.zeros((rows, tq), F32)
            for jp in range(n_sel):
                other = imp[jp:jp + 1, :]
                ge = lambda: jnp.where(other >= mine, 1.0, 0.0)
                gt_ = lambda: jnp.where(other > mine, 1.0, 0.0)
                if jp < r0:
                    rank = rank + ge()
                elif jp >= r0 + rows:
                    rank = rank + gt_()
                else:
                    rank = rank + jnp.where(row_id > jp, ge(), gt_())
            ranks.append(rank)
        rank = jnp.concatenate(ranks, axis=0)
        sel_bias = jnp.where(rank < float(SEL_TOPK), 0.0, NEG_INF).astype(BF16)
        pad = jnp.zeros((dh - n_sel, tq), BF16)
        q_augs.append([jnp.concatenate([head_rows(qr_ref, g), sel_bias, pad], axis=0) for g in range(grp)])
        o_cmps.append(o_cmp)

    sel, win = 0, 1

    def sel_tiles(kts):
        return [(sel, ksel_s, vsel_s, h2, kt, None) for kt in kts for h2 in range(2)]

    def sel_body(pair, carry):
        run_tiles(sel_tiles([2 * pair, 2 * pair + 1]), q_augs)
        return carry

    lax.fori_loop(0, qi // 2, sel_body, 0)

    @pl.when(qi % 2 == 1)
    def _():
        run_tiles(sel_tiles([qi - 1]), q_augs)

    run_tiles([(sel, ksel_s, vsel_s, h2, qi, "causal") for h2 in range(2)]
              + [(win, kwin_s, vwin_s, h2, qi - 2, "window_start") for h2 in range(2)]
              + [(win, kwin_s, vwin_s, h2, qi - 1, "in_sequence") for h2 in range(2)]
              + [(win, kwin_s, vwin_s, h2, qi, "causal") for h2 in range(2)], q_augs)

    for h2 in range(2):
        for g in range(grp):
            hg = h2 * grp + g
            gate = lambda branch: gt_ref[0, branch * heads + hg:branch * heads + hg + 1, :]
            o_ref[0, hg * dh:(hg + 1) * dh, :] = (
                gate(0) * o_cmps[h2][g] + gate(1) * result(sel, h2, g) + gate(2) * result(win, h2, g)
            ).astype(o_ref.dtype)


def _nsa_attention(q_t, qr_t, gates_t, kvc, kv):
    b, d, s = q_t.shape
    tq, tk = NSA_TQ, NSA_TK
    assert WINDOW == 2 * tq and tq % SEL_BLOCK == 0 and tq % tk == 0
    dh = NSA_DH
    pairs = NSA_KV_HEADS // 2
    heads = 2 * NSA_GROUP
    pw = heads * dh
    n_cmp_pad = kvc.shape[2]
    n_sel = s // SEL_BLOCK
    wt = jnp.asarray(_cmp_to_sel_weights_t(n_cmp_pad, n_sel), BF16)
    onehot = np.zeros((s, dh), np.float32)
    onehot[np.arange(s), np.arange(s) // SEL_BLOCK] = 1.0
    kv_spec = lambda grp: pl.BlockSpec((1, s, LANE), lambda bi, p, i: (bi, 0, pairs * grp + p))
    return pl.pallas_call(
        functools.partial(_nsa_kernel, tq=tq, tk=tk),
        out_shape=jax.ShapeDtypeStruct((b, d, s), BF16),
        grid=(b, pairs, s // tq),
        in_specs=[
            pl.BlockSpec((1, pw, tq), lambda bi, p, i: (bi, p, i)),
            pl.BlockSpec((1, pw, tq), lambda bi, p, i: (bi, p, i)),
            pl.BlockSpec((1, LANE, tq), lambda bi, p, i: (bi, p, i)),
            pl.BlockSpec((1, 2, n_cmp_pad, LANE), lambda bi, p, i: (bi, p, 0, 0)),
            kv_spec(0), kv_spec(1), kv_spec(2), kv_spec(3),
            _const_spec(wt.shape),
            _const_spec(onehot.shape),
        ],
        out_specs=pl.BlockSpec((1, pw, tq), lambda bi, p, i: (bi, p, i)),
        scratch_shapes=[
            pltpu.VMEM((2, s, 2 * dh), BF16),
            pltpu.VMEM((2, s, 2 * dh), BF16),
            pltpu.VMEM((2, s // tk, NSA_ACC_ROWS, tk), BF16),
            pltpu.VMEM((2, s // tk, NSA_ACC_ROWS, tk), BF16),
            pltpu.VMEM((2, dh, n_cmp_pad), BF16),
            pltpu.VMEM((2 * heads, 1, tq), F32),
            pltpu.VMEM((2 * heads, NSA_ACC_ROWS, tq), F32),
        ],
        compiler_params=_params("parallel", "parallel", "arbitrary"),
    )(q_t, qr_t, gates_t, kvc, kv, kv, kv, kv, wt, jnp.asarray(onehot, BF16))


def _rope_tables(seq, width):
    half = NSA_DH // 2
    inv = ROPE_THETA ** (-jnp.arange(half, dtype=F32) / half)
    ang = jnp.arange(seq, dtype=F32)[:, None] * inv[None, :]
    cos, sin = jnp.cos(ang), jnp.sin(ang)
    reps = width // NSA_DH
    cos_full = jnp.tile(jnp.concatenate([cos, cos], axis=1), (1, reps))
    sin_signed = jnp.tile(jnp.concatenate([-sin, sin], axis=1), (1, reps))
    return cos_full, sin_signed


def _nsa_gate_weights(w_gate):
    d = w_gate.shape[0]
    pairs = NSA_KV_HEADS // 2
    heads = 2 * NSA_GROUP
    wg = w_gate.reshape(d, 3, pairs, heads).transpose(0, 2, 1, 3).reshape(d, pairs, 3 * heads)
    return jnp.pad(wg, ((0, 0), (0, 0), (0, LANE - 3 * heads))).reshape(d, pairs * LANE)


def kernel(x, norm_mix, norm_ffn, gla_w_in, gla_w_alpha_up, gla_b_alpha, gla_norm, gla_w_o, kv_norm, nsa_w_kv, cmp_pe_k, cmp_pe_v, cmp_k_w1, cmp_k_w2, cmp_v_w1, cmp_v_w2, nsa_w_in, nsa_w_o, ffn_w_up, ffn_conv_w, ffn_conv_b, ffn_w_down, norm_final):
    b, s, d = x.shape
    n = b * s
    x2 = x.reshape(n, d)

    n_main = 2 * GLA_HEADS * GLA_DK + 2 * GLA_HEADS * GLA_DV
    w_in = gla_w_in[0]
    n_pairs = len(GATE_PIECE_PAIRS)
    lane_pad = LANE - n_pairs * GLA_GATE_RANK
    w_low = jnp.pad(jnp.tile(w_in[:, n_main:], (1, n_pairs)), ((0, 0), (0, lane_pad))).astype(BF16)
    w_up_pieces = _split3(gla_w_alpha_up[0])
    w_up6 = jnp.pad(jnp.concatenate([w_up_pieces[iw] for _, iw in GATE_PIECE_PAIRS], axis=0),
                    ((0, lane_pad), (0, 0)))
    proj, log_a = _gla_front(x2, norm_mix[0], w_in[:, :n_main].astype(BF16), w_low, w_up6, gla_b_alpha[0])
    o = _gla_core(proj.reshape(b, s, n_main), log_a.reshape(b, s, -1), gla_norm[0])
    x2 = _proj_res(o.reshape(n, -1), gla_w_o[0].astype(BF16), x2)
    x2 = _conv_ffn(x2.reshape(b, s, d), norm_ffn[0], ffn_w_up[0].astype(BF16), ffn_conv_w[0], ffn_conv_b[0],
                   ffn_w_down[0].astype(BF16)).reshape(n, d)

    hk, dh = NSA_KV_HEADS, NSA_DH
    nq = NSA_HEADS * dh
    w_nsa = nsa_w_in[0]
    kv, k16, v16, q_t, qr_t, gates_t = _nsa_front(
        x2, kv_norm, norm_mix[1], nsa_w_kv.astype(BF16), w_nsa[:, :nq].astype(BF16),
        _nsa_gate_weights(w_nsa[:, nq:]).astype(BF16), _rope_tables(s, hk * dh), seq=s,
        q_scale=dh ** -0.5 * LOG2E)
    kvc = _compress(k16, v16, cmp_pe_k, cmp_pe_v, cmp_k_w1.astype(BF16), cmp_v_w1.astype(BF16),
                    cmp_k_w2.astype(BF16), cmp_v_w2.astype(BF16), rows_per_seq=s // CMP_STRIDE)

    o_t = _nsa_attention(q_t, qr_t, gates_t, kvc, kv.reshape(b, s, -1))
    x2 = _proj_res(o_t, nsa_w_o[0].astype(BF16), x2, transposed=True)
    return _conv_ffn(x2.reshape(b, s, d), norm_ffn[1], ffn_w_up[1].astype(BF16), ffn_conv_w[1], ffn_conv_b[1],
                     ffn_w_down[1].astype(BF16), norm_final)
```

```python
import functools
import math

import numpy as np
import jax
import jax.numpy as jnp
from jax import lax
from jax.experimental import pallas as pl
from jax.experimental.pallas import tpu as pltpu

F32 = jnp.float32
BF16 = jnp.bfloat16

EPS = 1e-6
NEG_INF = -1e30
ROPE_THETA = 10000.0
LOG2E = math.log2(math.e)

LANE = 128
BF16_SUBLANES = 16
VMEM_LIMIT_BYTES = 56 * 1024 * 1024

GLA_HEADS = 4
GLA_DK = 128
GLA_DV = 256
GLA_GATE_RANK = 16
GLA_TAU = 16.0
GLA_CHUNK = 64
GLA_LEVEL_GROUP = 4
GATE_PIECE_PAIRS = ((0, 0), (0, 1), (1, 0), (0, 2), (1, 1), (2, 0))
NSA_HEADS = 16
NSA_KV_HEADS = 4
NSA_GROUP = 4
NSA_DH = 64
CMP_BLOCK = 32
CMP_STRIDE = 16
CMP_HIDDEN = 256
SEL_BLOCK = 64
SEL_TOPK = 16
WINDOW = 512
FORCE_SCORE = 1e4
CONV_WIDTH = 3
CONV_HALO = 16

PROJ_ROWS = 1024
PROJ_COLS = 512
RES_ROWS = 1024
GLA_ROWS = 1024
GLA_CHUNKS_PER_TRIP = 4
FFN_ROWS = 1024
FFN_COLS = 256
FFN_TILES_PER_TRIP = 4
CMP_ROWS = 512
NSA_TQ = 256
NSA_TK = 256
NSA_SEL_TILES_PER_TRIP = 4
NSA_ACC_ROWS = NSA_DH + BF16_SUBLANES


def _dot(a, b):
    return jnp.dot(a, b, preferred_element_type=F32)


def _dot_nt(a, b):
    return lax.dot_general(a, b, (((1,), (1,)), ((), ())), preferred_element_type=F32)


def _dot_tn(a, b):
    return lax.dot_general(a, b, (((0,), (0,)), ((), ())), preferred_element_type=F32)


def _rms(x, g):
    return x * lax.rsqrt(jnp.mean(x * x, axis=-1, keepdims=True) + EPS) * g


def _split3(x):
    hi = x.astype(BF16)
    r1 = x - hi.astype(F32)
    mid = r1.astype(BF16)
    lo = (r1 - mid.astype(F32)).astype(BF16)
    return hi, mid, lo


def _rope_tile(x, cos, sin_signed):
    w = x.shape[-1]
    lane = lax.broadcasted_iota(jnp.int32, x.shape, 1)
    first_half = (lane & (NSA_DH - 1)) < (NSA_DH // 2)
    partner = jnp.where(first_half, pltpu.roll(x, w - NSA_DH // 2, 1), pltpu.roll(x, NSA_DH // 2, 1))
    return x * cos + partner * sin_signed


def _params(*sem):
    return pltpu.CompilerParams(dimension_semantics=sem, vmem_limit_bytes=VMEM_LIMIT_BYTES)


def _const_spec(shape):
    nd = len(shape)
    return pl.BlockSpec(shape, lambda *_: (0,) * nd, pipeline_mode=pl.Buffered(1))


def _gla_front_kernel(x_ref, g_ref, w_ref, wl_ref, wu_ref, b_ref, o_ref, la_ref):
    hn = _rms(x_ref[...], g_ref[...]).astype(BF16)
    a_rep = _dot(hn, wl_ref[...])
    hi, mid, lo = _split3(a_rep)
    group = lax.broadcasted_iota(jnp.int32, a_rep.shape, 1) // GLA_GATE_RANK
    a_piece = [g for g, (ia, _) in enumerate(GATE_PIECE_PAIRS)]
    is_piece = lambda p: functools.reduce(
        jnp.logical_or, [group == g for g in a_piece if GATE_PIECE_PAIRS[g][0] == p])
    a6 = jnp.where(is_piece(2), lo, jnp.where(is_piece(1), mid, hi))
    z = b_ref[...] + _dot(a6, wu_ref[...])
    log_sig = jnp.minimum(z, 0.0) - jnp.log(1.0 + jnp.exp(-jnp.abs(z)))
    la_ref[...] = log_sig * (1.0 / GLA_TAU)
    m = w_ref.shape[1]
    for c in range(0, m, PROJ_COLS):
        o_ref[:, c:c + PROJ_COLS] = _dot(hn, w_ref[:, c:c + PROJ_COLS]).astype(o_ref.dtype)


def _gla_front(x2, g, w_main, w_low, w_up6, b):
    n, d = x2.shape
    m = w_main.shape[1]
    hk = w_up6.shape[-1]
    tm = PROJ_ROWS
    return pl.pallas_call(
        _gla_front_kernel,
        out_shape=[jax.ShapeDtypeStruct((n, m), BF16), jax.ShapeDtypeStruct((n, hk), F32)],
        grid=(n // tm,),
        in_specs=[
            pl.BlockSpec((tm, d), lambda i: (i, 0)),
            _const_spec((1, d)),
            _const_spec((d, m)),
            _const_spec((d, LANE)),
            _const_spec((LANE, hk)),
            _const_spec((1, hk)),
        ],
        out_specs=[pl.BlockSpec((tm, m), lambda i: (i, 0)), pl.BlockSpec((tm, hk), lambda i: (i, 0))],
        compiler_params=_params("parallel"),
    )(x2, g.reshape(1, d), w_main, w_low, w_up6, b.reshape(1, hk))


def _gla_constants():
    c = GLA_CHUNK
    idx = np.arange(c)
    row, col = idx[:, None], idx[None, :]
    mats = [col <= row, col > row]
    masks = []
    h = 1
    while h < c:
        blk = idx // (2 * h)
        upper = (idx % (2 * h)) >= h
        r = (blk * 2 * h + h - 1)[:, None]
        mats.append(np.where(upper[:, None], (col > r) & (col <= row), (col > row) & (col <= r)))
        masks.append((blk[:, None] == blk[None, :]) & upper[:, None] & (~upper[None, :]))
        h *= 2
    masks.append(np.eye(c, dtype=bool))
    strips = np.zeros((len(masks), c, LANE), np.float32)
    for lvl, m in enumerate(masks):
        half = (lvl % GLA_LEVEL_GROUP) % (LANE // c)
        strips[lvl, :, half * c:(half + 1) * c] = m
    return np.concatenate(mats, 0).astype(np.float32), strips


def _gla_kernel(q_ref, k_ref, v_ref, r_ref, g_ref, gs_ref, mk_ref, gn_ref, o_ref, st_ref, *, n_chunks):
    c, dk, dv = GLA_CHUNK, GLA_DK, GLA_DV
    n_levels = mk_ref.shape[0]
    per_strip = LANE // c
    q_scale = dk ** -0.5

    @pl.when(pl.program_id(1) == 0)
    def _():
        st_ref[...] = jnp.zeros_like(st_ref)

    def decays(h, r0, t):
        ks = slice(h * dk, (h + 1) * dk)
        t["q"] = q_ref[0, pl.ds(r0, c), ks].astype(F32) * q_scale
        t["k"] = k_ref[0, pl.ds(r0, c), ks].astype(F32)
        t["v"] = v_ref[0, pl.ds(r0, c), h * dv:(h + 1) * dv]
        hi, mid, _ = _split3(g_ref[0, pl.ds(r0, c), ks])
        t["e2"] = _dot(gs_ref[...], jnp.concatenate([hi, mid], axis=1))

    def level_products(t):
        e2 = t.pop("e2")
        x = t["x"] = jnp.exp(e2[:, :dk] + e2[:, dk:])
        q, k = t["q"], t["k"]
        t["prods"] = []
        for first in range(0, n_levels, GLA_LEVEL_GROUP):
            lvls = list(range(first, min(first + GLA_LEVEL_GROUP, n_levels)))
            factor = lambda lvl: x[(2 + lvl) * c:(3 + lvl) * c] if lvl < n_levels - 1 else 1.0
            q_stack = jnp.concatenate([(q * factor(lvl)).astype(BF16) for lvl in lvls], axis=0)
            k_stack = jnp.concatenate([(k * factor(lvl)).astype(BF16) for lvl in lvls]
                                      + [jnp.zeros((c, dk), BF16)] * (GLA_LEVEL_GROUP - len(lvls)), axis=0)
            t["prods"].append((lvls, _dot_nt(q_stack, k_stack)))
        t["q_inter"] = (q * x[0:c]).astype(BF16)
        t["k_state"] = (k * x[c:2 * c]).astype(BF16)

    def outputs(h, t):
        strip = jnp.zeros((c, LANE), F32)
        for lvls, prod in t.pop("prods"):
            for j, lvl in enumerate(lvls):
                col = (j // per_strip) * LANE
                strip = strip + prod[j * c:(j + 1) * c, col:col + LANE] * mk_ref[lvl]
        att = strip[:, 0:c]
        for part in range(1, per_strip):
            att = att + strip[:, part * c:(part + 1) * c]
        st = st_ref[h]
        t["o"] = _dot(att.astype(BF16), t["v"]) + _dot_nt(t["q_inter"], st.astype(BF16))
        st_ref[h] = st * t["x"][c - 1:c] + _dot_tn(t["v"], t["k_state"])

    def finish(h, r0, t):
        o = t["o"]
        o = o * lax.rsqrt(jnp.mean(o * o, axis=-1, keepdims=True) + EPS) * gn_ref[h:h + 1, :]
        r = r_ref[0, pl.ds(r0, c), h * dv:(h + 1) * dv].astype(F32)
        o_ref[0, pl.ds(r0, c), h * dv:(h + 1) * dv] = (o * (r * jax.nn.sigmoid(r))).astype(o_ref.dtype)

    def body(ci, carry):
        items = [(h, pl.multiple_of((ci * GLA_CHUNKS_PER_TRIP + sub) * c, c), dict())
                 for sub in range(GLA_CHUNKS_PER_TRIP) for h in range(GLA_HEADS)]
        for h, r0, t in items:
            decays(h, r0, t)
        for h, r0, t in items:
            level_products(t)
        for h, r0, t in items:
            outputs(h, t)
        for h, r0, t in items:
            finish(h, r0, t)
        return carry

    lax.fori_loop(0, n_chunks // GLA_CHUNKS_PER_TRIP, body, 0)


def _gla_core(proj, log_a, gla_norm):
    b, s, _ = proj.shape
    h, dk, dv = GLA_HEADS, GLA_DK, GLA_DV
    ts = GLA_ROWS
    gs, mk = _gla_constants()
    return pl.pallas_call(
        functools.partial(_gla_kernel, n_chunks=ts // GLA_CHUNK),
        out_shape=jax.ShapeDtypeStruct((b, s, h * dv), BF16),
        grid=(b, s // ts),
        in_specs=[
            pl.BlockSpec((1, ts, h * dk), lambda i, j: (i, j, 0)),
            pl.BlockSpec((1, ts, h * dk), lambda i, j: (i, j, 1)),
            pl.BlockSpec((1, ts, h * dv), lambda i, j: (i, j, 1)),
            pl.BlockSpec((1, ts, h * dv), lambda i, j: (i, j, 2)),
            pl.BlockSpec((1, ts, h * dk), lambda i, j: (i, j, 0)),
            _const_spec(gs.shape),
            _const_spec(mk.shape),
            _const_spec((h, dv)),
        ],
        out_specs=pl.BlockSpec((1, ts, h * dv), lambda i, j: (i, j, 0)),
        scratch_shapes=[pltpu.VMEM((h, dv, dk), F32)],
        compiler_params=_params("parallel", "arbitrary"),
    )(proj, proj, proj, proj, log_a, jnp.asarray(gs, BF16), jnp.asarray(mk), gla_norm)


def _proj_res_kernel(a_ref, w_ref, r_ref, o_ref):
    o_ref[...] = r_ref[...] + _dot(a_ref[...], w_ref[...])


def _proj_res_t_kernel(a_ref, w_ref, r_ref, o_ref):
    o_ref[...] = r_ref[...] + _dot_tn(a_ref[0], w_ref[...])


def _proj_res(a, w, res, *, transposed=False):
    n, d = res.shape
    k = w.shape[0]
    tm = RES_ROWS
    if transposed:
        per_seq = a.shape[2] // tm
        a_spec = pl.BlockSpec((1, k, tm), lambda i: (i // per_seq, 0, i % per_seq))
    else:
        a_spec = pl.BlockSpec((tm, k), lambda i: (i, 0))
    return pl.pallas_call(
        _proj_res_t_kernel if transposed else _proj_res_kernel,
        out_shape=jax.ShapeDtypeStruct((n, d), F32),
        grid=(n // tm,),
        in_specs=[
            a_spec,
            _const_spec((k, d)),
            pl.BlockSpec((tm, d), lambda i: (i, 0)),
        ],
        out_specs=pl.BlockSpec((tm, d), lambda i: (i, 0)),
        compiler_params=_params("parallel"),
    )(a, w, res)


def _ffn_kernel(*refs, tm, final):
    if final:
        xm_ref, xh_ref, gn_ref, wup_ref, cw_ref, cb_ref, wd_ref, gf_ref, o_ref, hn_ref, u_ref = refs
    else:
        xm_ref, xh_ref, gn_ref, wup_ref, cw_ref, cb_ref, wd_ref, o_ref, hn_ref, u_ref = refs
    halo = CONV_HALO
    nf = wd_ref.shape[0]
    g = gn_ref[...]
    hn_ref[0:halo, :] = _rms(xh_ref[0], g).astype(BF16)
    hn_ref[halo:, :] = _rms(xm_ref[0], g).astype(BF16)
    o_ref[0] = xm_ref[0]
    keep = jnp.where(pl.program_id(1) > 0, 1.0, 0.0)

    def up_project(fi, slot):
        hn = hn_ref[...]
        for gv in range(2):
            tf = u_ref.shape[-1]
            col = pl.multiple_of((gv * nf + fi) * tf, tf)
            u = _dot(hn, wup_ref[:, pl.ds(col, tf)])
            u_ref[slot, gv] = u
            u_ref[slot, gv, 0:halo, :] = u[0:halo] * keep

    def activation(fi, slot):
        outs = []
        for gv in range(2):
            out = cb_ref[gv, fi]
            for tap in range(CONV_WIDTH):
                start = halo - (CONV_WIDTH - 1) + tap
                out = out + u_ref[slot, gv, start:start + tm, :] * cw_ref[gv, fi, tap:tap + 1, :]
            outs.append(out)
        gate, val = outs
        return (gate * jax.nn.sigmoid(gate) * val).astype(BF16)

    def run(tiles):
        up_project(tiles[0], 0)
        for j, fi in enumerate(tiles):
            if j + 1 < len(tiles):
                up_project(tiles[j + 1], (j + 1) % 2)
            o_ref[0] += _dot(activation(fi, j % 2), wd_ref[fi])

    group = FFN_TILES_PER_TRIP

    def trip(ti, carry):
        run([ti * group + j for j in range(group)])
        return carry

    lax.fori_loop(0, nf // group, trip, 0)
    if nf % group:
        run(list(range(nf - nf % group, nf)))
    if final:
        o_ref[0] = _rms(o_ref[0], gf_ref[...])


def _conv_ffn(x3, g, w_up, conv_w, conv_b, w_down, final_gain=None):
    b, s, d = x3.shape
    ffn = w_down.shape[0]
    tm, tf = FFN_ROWS, FFN_COLS
    nf = ffn // tf
    halo = CONV_HALO
    final = final_gain is not None
    w_up_t = w_up
    conv_w_t = conv_w.reshape(CONV_WIDTH, 2, nf, tf).transpose(1, 2, 0, 3)
    conv_b_t = conv_b.reshape(2, nf, 1, tf)
    w_down_t = w_down.reshape(nf, tf, d)
    in_specs = [
        pl.BlockSpec((1, tm, d), lambda bi, i: (bi, i, 0)),
        pl.BlockSpec((1, halo, d), lambda bi, i: (bi, jnp.maximum(i * (tm // halo) - 1, 0), 0)),
        _const_spec((1, d)),
        _const_spec(w_up_t.shape),
        _const_spec(conv_w_t.shape),
        _const_spec(conv_b_t.shape),
        _const_spec(w_down_t.shape),
    ]
    args = [x3, x3, g.reshape(1, d), w_up_t, conv_w_t, conv_b_t, w_down_t]
    if final:
        in_specs.append(_const_spec((1, d)))
        args.append(final_gain.reshape(1, d))
    return pl.pallas_call(
        functools.partial(_ffn_kernel, tm=tm, final=final),
        out_shape=jax.ShapeDtypeStruct((b, s, d), F32),
        grid=(b, s // tm),
        in_specs=in_specs,
        out_specs=pl.BlockSpec((1, tm, d), lambda bi, i: (bi, i, 0)),
        scratch_shapes=[
            pltpu.VMEM((tm + halo, d), BF16),
            pltpu.VMEM((2, 2, tm + halo, tf), F32),
        ],
        compiler_params=_params("parallel", "parallel"),
    )(*args)


def _nsa_front_kernel(x_ref, gkv_ref, gq_ref, wkv_ref, wq_ref, wg_ref, cos_ref, sin_ref,
                      kv_ref, k16_ref, v16_ref, q_ref, qr_ref, gt_ref, raw_s, *, q_scale):
    x = x_ref[...]
    y = x * lax.rsqrt(jnp.mean(x * x, axis=-1, keepdims=True) + EPS)
    hn_kv = (y * gkv_ref[...]).astype(BF16)
    hn_q = (y * gq_ref[...]).astype(BF16)
    cos, sin = cos_ref[...], sin_ref[...]
    rope_w = cos.shape[1]

    gt_ref[0] = jax.nn.sigmoid(_dot(hn_q, wg_ref[...])).T
    for grp in range(wq_ref.shape[1] // rope_w):
        acc = _dot(hn_q, wq_ref[:, grp * rope_w:(grp + 1) * rope_w]) * q_scale
        q_ref[0, grp * rope_w:(grp + 1) * rope_w, :] = acc.T.astype(q_ref.dtype)
        qr_ref[0, grp * rope_w:(grp + 1) * rope_w, :] = _rope_tile(acc, cos, sin).T.astype(qr_ref.dtype)
    n_raw = 2
    for grp in (2, 4, 3, 5):
        acc = _dot(hn_kv, wkv_ref[:, grp * rope_w:(grp + 1) * rope_w])
        if grp in (2, 4):
            acc = _rope_tile(acc, cos, sin)
        kv_ref[:, (grp - n_raw) * rope_w:(grp - n_raw + 1) * rope_w] = acc.astype(kv_ref.dtype)
    strips, groups = raw_s.shape[0], raw_s.shape[1] // CMP_STRIDE
    for grp, out_ref in ((0, k16_ref), (1, v16_ref)):
        acc = _dot(hn_kv, wkv_ref[:, grp * rope_w:(grp + 1) * rope_w])
        for j in range(strips):
            raw_s[j] = acc[:, j * LANE:(j + 1) * LANE]
        for r in range(CMP_STRIDE):
            for j in range(strips):
                out_ref[:, r * rope_w + j * LANE:r * rope_w + (j + 1) * LANE] = (
                    raw_s[j, pl.ds(r, groups, stride=CMP_STRIDE), :].astype(out_ref.dtype))


def _nsa_front(x2, g_kv, g_q, w_kv, w_q, w_g, rope, *, seq, q_scale):
    n, d = x2.shape
    tm = PROJ_ROWS
    per_seq = seq // tm
    batch = n // seq
    rope_w = rope[0].shape[1]
    row = lambda m: pl.BlockSpec((tm, m), lambda i: (i, 0))
    col = lambda m: pl.BlockSpec((1, m, tm), lambda i: (i // per_seq, 0, i % per_seq))
    n_att = w_kv.shape[1] - 2 * rope_w
    grouped = CMP_STRIDE * rope_w
    return pl.pallas_call(
        functools.partial(_nsa_front_kernel, q_scale=q_scale),
        out_shape=[
            jax.ShapeDtypeStruct((n, n_att), BF16),
            jax.ShapeDtypeStruct((n // CMP_STRIDE, grouped), BF16),
            jax.ShapeDtypeStruct((n // CMP_STRIDE, grouped), BF16),
            jax.ShapeDtypeStruct((batch, w_q.shape[1], seq), BF16),
            jax.ShapeDtypeStruct((batch, w_q.shape[1], seq), BF16),
            jax.ShapeDtypeStruct((batch, w_g.shape[1], seq), F32),
        ],
        grid=(n // tm,),
        in_specs=[
            row(d), _const_spec((1, d)), _const_spec((1, d)),
            _const_spec(w_kv.shape), _const_spec(w_q.shape), _const_spec(w_g.shape),
            pl.BlockSpec((tm, rope_w), lambda i: (i % per_seq, 0)),
            pl.BlockSpec((tm, rope_w), lambda i: (i % per_seq, 0)),
        ],
        out_specs=[row(n_att),
                   pl.BlockSpec((tm // CMP_STRIDE, grouped), lambda i: (i, 0)),
                   pl.BlockSpec((tm // CMP_STRIDE, grouped), lambda i: (i, 0)),
                   col(w_q.shape[1]), col(w_q.shape[1]), col(w_g.shape[1])],
        scratch_shapes=[pltpu.VMEM((rope_w // LANE, tm, LANE), F32)],
        compiler_params=_params("parallel"),
    )(x2, g_kv.reshape(1, d), g_q.reshape(1, d), w_kv, w_q, w_g, *rope)


def _compress_kernel(tk_ref, tv_ref, w1k_ref, w1v_ref, bk_ref, bv_ref, w2k_ref, w2v_ref, o_ref):
    rows = tk_ref.shape[0]
    dh = NSA_DH
    width = NSA_KV_HEADS * dh
    row = lax.broadcasted_iota(jnp.int32, (rows, 1), 0)
    valid = jnp.where(row < rows - 1, 1.0, 0.0)

    def first_layer(t_ref, w1_ref, h):
        y = None
        for r in range(CMP_STRIDE):
            part = _dot(t_ref[:, r * width + h * dh:r * width + (h + 1) * dh], w1_ref[r * dh:(r + 1) * dh, :])
            y = part if y is None else y + part
        return y

    def second_layer(y, b_ref, w2_ref):
        hid = y.shape[1] // 2
        pre = y[:, :hid] + pltpu.roll(y[:, hid:], rows - 1, 0) + b_ref[0:1, :]
        return _dot(jax.nn.gelu(pre).astype(BF16), w2_ref[...]) * valid

    for h in range(NSA_KV_HEADS):
        yk = first_layer(tk_ref, w1k_ref, h)
        yv = first_layer(tv_ref, w1v_ref, h)
        o_ref[0, h] = jnp.concatenate(
            [second_layer(yk, bk_ref, w2k_ref), second_layer(yv, bv_ref, w2v_ref)], axis=1).astype(o_ref.dtype)


def _pe_bias_kernel(pk_ref, pv_ref, w1k_ref, w1v_ref, ok_ref, ov_ref):
    ok_ref[...] = _dot(pk_ref[...], w1k_ref[...])
    ov_ref[...] = _dot(pv_ref[...], w1v_ref[...])


def _compress(tk, tv, pe_k, pe_v, w1k, w1v, w2k, w2v, *, rows_per_seq):
    r, width = tk.shape
    hid = w1k.shape[1]
    dh = w2k.shape[1]
    kdim = CMP_STRIDE * dh
    batch = r // rows_per_seq
    sub = 8
    pk = jnp.broadcast_to(pe_k.reshape(1, -1), (sub, 2 * kdim)).astype(BF16)
    pv = jnp.broadcast_to(pe_v.reshape(1, -1), (sub, 2 * kdim)).astype(BF16)
    bk, bv = pl.pallas_call(
        _pe_bias_kernel,
        out_shape=[jax.ShapeDtypeStruct((sub, hid), F32)] * 2,
    )(pk, pv, w1k, w1v)
    w1k_cat = jnp.concatenate([w1k[:kdim], w1k[kdim:]], axis=1)
    w1v_cat = jnp.concatenate([w1v[:kdim], w1v[kdim:]], axis=1)
    return pl.pallas_call(
        _compress_kernel,
        out_shape=jax.ShapeDtypeStruct((batch, NSA_KV_HEADS, rows_per_seq, 2 * dh), BF16),
        grid=(batch,),
        in_specs=[
            pl.BlockSpec((rows_per_seq, width), lambda i: (i, 0)),
            pl.BlockSpec((rows_per_seq, width), lambda i: (i, 0)),
            _const_spec((kdim, 2 * hid)),
            _const_spec((kdim, 2 * hid)),
            _const_spec((sub, hid)),
            _const_spec((sub, hid)),
            _const_spec((hid, dh)),
            _const_spec((hid, dh)),
        ],
        out_specs=pl.BlockSpec((1, NSA_KV_HEADS, rows_per_seq, 2 * dh), lambda i: (i, 0, 0, 0)),
        compiler_params=_params("parallel"),
    )(tk, tv, w1k_cat, w1v_cat, bk, bv, w2k, w2v)


def _cmp_to_sel_weights_t(n_cmp_pad, n_sel):
    c0 = np.arange(n_cmp_pad)[None, :] * CMP_STRIDE
    s0 = np.arange(n_sel)[:, None] * SEL_BLOCK
    ov = np.clip(np.minimum(c0 + CMP_BLOCK, s0 + SEL_BLOCK) - np.maximum(c0, s0), 0, None)
    return (ov / CMP_BLOCK).astype(np.float32)


def _nsa_kernel(q_ref, qr_ref, gt_ref, kc_ref, ks_ref, vs_ref, kw_ref, vw_ref, wt_ref, oh_ref, o_ref,
                ksel_s, kwin_s, vsel_s, vwin_s, vcmp_s, m_s, acc_s, *, tq, tk):
    grp, dh = NSA_GROUP, NSA_DH
    heads = 2 * grp
    qi = pl.program_id(2)
    t0 = qi * tq
    n_sel = wt_ref.shape[0]
    n_cmp_pad = wt_ref.shape[1]
    seq = ks_ref.shape[1]

    @pl.when(qi == 0)
    def _():
        ones_row = jnp.where(lax.broadcasted_iota(jnp.int32, (BF16_SUBLANES, tk), 0) == 0, 1.0, 0.0).astype(BF16)
        for h2 in range(2):
            cols = slice(h2 * dh, (h2 + 1) * dh)
            ksel_s[h2, :, 0:dh] = ks_ref[0, :, cols]
            ksel_s[h2, :, dh:2 * dh] = oh_ref[...]
            kwin_s[h2, :, 0:dh] = kw_ref[0, :, cols]
            kwin_s[h2, :, dh:2 * dh] = jnp.zeros((seq, dh), BF16)
            vcmp_s[h2] = kc_ref[0, h2, :, dh:2 * dh].astype(F32).T.astype(BF16)

        def build(j, carry):
            r0 = pl.multiple_of(j * tk, tk)
            for src, dst in ((vs_ref, vsel_s), (vw_ref, vwin_s)):
                blk_t = src[0, pl.ds(r0, tk), :].astype(F32).T
                for h2 in range(2):
                    dst[h2, j, 0:dh, :] = blk_t[h2 * dh:(h2 + 1) * dh].astype(BF16)
                    dst[h2, j, dh:, :] = ones_row
            return carry

        lax.fori_loop(0, seq // tk, build, 0)

    tpos = t0 + lax.broadcasted_iota(jnp.int32, (1, tq), 1)
    key_iota = lax.broadcasted_iota(jnp.int32, (tk, 1), 0)
    cmp_end = lax.broadcasted_iota(jnp.int32, (n_cmp_pad, 1), 0) * CMP_STRIDE + (CMP_BLOCK - 1)
    cmp_mask = cmp_end <= tpos
    blk = lax.broadcasted_iota(jnp.int32, (n_sel, 1), 0)
    cur = tpos >> 6
    forced = (blk == 0) | (blk == cur) | (blk == cur - 1)
    win_lo = jnp.maximum(tpos - WINDOW, -1)

    def run_tiles(tiles, q_augs):
        chunks = []
        for sub in range(tq // tk):
            for branch, k_s, v_s, h2, tile, mode in tiles:
                kt = jnp.maximum(tile, 0) * (tq // tk) + sub
                k0 = pl.multiple_of(kt * tk, tk)
                shared = dict(k=k_s[h2, pl.ds(k0, tk), :], v=v_s[h2, kt], mode=mode,
                              kpos=tile * tq + sub * tk + key_iota)
                for g in range(grp):
                    chunks.append(dict(shared, slot=(h2 * 2 + branch) * grp + g, q=q_augs[h2][g]))

        def scores(c):
            s = _dot(c["k"], c["q"])
            if c["mode"] == "causal":
                s = jnp.where(c["kpos"] <= tpos, s, NEG_INF)
            elif c["mode"] == "window_start":
                s = jnp.where(c["kpos"] > win_lo, s, NEG_INF)
            elif c["mode"] == "in_sequence":
                s = jnp.where(c["kpos"] >= 0, s, NEG_INF)
            c["s"] = s

        def softmax_update(c):
            m = m_s[c["slot"]]
            m_new = jnp.maximum(m, jnp.max(c["s"], axis=0, keepdims=True))
            c["p"] = jnp.exp2(c.pop("s") - m_new).astype(BF16)
            c["alpha"] = jnp.exp2(m - m_new)
            m_s[c["slot"]] = m_new

        def accumulate(c):
            acc_s[c["slot"]] = c["alpha"] * acc_s[c["slot"]] + _dot(c["v"], c.pop("p"))

        ahead = 6
        for c in chunks[:ahead]:
            scores(c)
        for i, c in enumerate(chunks):
            softmax_update(c)
            if i + ahead < len(chunks):
                scores(chunks[i + ahead])
            accumulate(c)

    def result(branch, h2, g):
        slot = (h2 * 2 + branch) * grp + g
        return acc_s[slot, 0:dh, :] / acc_s[slot, dh:dh + 1, :]

    m_s[...] = jnp.full(m_s.shape, NEG_INF, F32)
    acc_s[...] = jnp.zeros(acc_s.shape, F32)
    q_augs, o_cmps = [], []
    cmp_scores = [[_dot(kc_ref[0, h2, :, 0:dh], q_ref[0, (h2 * grp + g) * dh:(h2 * grp + g + 1) * dh, :])
                   for g in range(grp)] for h2 in range(2)]

    for h2 in range(2):
        head_rows = lambda ref, g: ref[0, (h2 * grp + g) * dh:(h2 * grp + g + 1) * dh, :]

        p_grp = jnp.zeros((n_cmp_pad, tq), F32)
        o_cmp = []
        for g in range(grp):
            s = jnp.where(cmp_mask, cmp_scores[h2][g], NEG_INF)
            e = jnp.where(cmp_mask, jnp.exp2(s - jnp.max(s, axis=0, keepdims=True)), 0.0)
            inv = 1.0 / jnp.maximum(jnp.sum(e, axis=0, keepdims=True), 1e-30)
            o_cmp.append(_dot(vcmp_s[h2], e.astype(BF16)) * inv)
            p_grp = p_grp + e * inv
        i3 = _dot(wt_ref[...], jnp.concatenate(_split3(p_grp), axis=1))
        imp = i3[:, 0:tq] + i3[:, tq:2 * tq] + i3[:, 2 * tq:]
        imp = jnp.where(blk > cur, -1.0, jnp.where(forced, FORCE_SCORE, imp))
        rows = 8
        ranks = []
        for r0 in range(0, n_sel, rows):
            mine = imp[r0:r0 + rows]
            row_id = blk[r0:r0 + rows]
            rank = jnp.zeros((rows, tq), F32)
            for jp in range(n_sel):
                other = imp[jp:jp + 1, :]
                ge = lambda: jnp.where(other >= mine, 1.0, 0.0)
                gt_ = lambda: jnp.where(other > mine, 1.0, 0.0)
                if jp < r0:
                    rank = rank + ge()
                elif jp >= r0 + rows:
                    rank = rank + gt_()
                else:
                    rank = rank + jnp.where(row_id > jp, ge(), gt_())
            ranks.append(rank)
        rank = jnp.concatenate(ranks, axis=0)
        sel_bias = jnp.where(rank < float(SEL_TOPK), 0.0, NEG_INF).astype(BF16)
        pad = jnp.zeros((dh - n_sel, tq), BF16)
        q_augs.append([jnp.concatenate([head_rows(qr_ref, g), sel_bias, pad], axis=0) for g in range(grp)])
        o_cmps.append(o_cmp)

    sel, win = 0, 1

    def sel_tiles(kts):
        return [(sel, ksel_s, vsel_s, h2, kt, None) for kt in kts for h2 in range(2)]

    group = NSA_SEL_TILES_PER_TRIP

    def sel_body(trip, carry):
        run_tiles(sel_tiles([trip * group + j for j in range(group)]), q_augs)
        return carry

    lax.fori_loop(0, qi // group, sel_body, 0)
    done = (qi // group) * group
    size = group // 2
    while size >= 1:
        @pl.when(((qi - done) & size) != 0)
        def _(done=done, size=size):
            run_tiles(sel_tiles([done + j for j in range(size)]), q_augs)

        done = done + ((qi - done) & size)
        size //= 2

    run_tiles([(sel, ksel_s, vsel_s, h2, qi, "causal") for h2 in range(2)]
              + [(win, kwin_s, vwin_s, h2, qi - 2, "window_start") for h2 in range(2)]
              + [(win, kwin_s, vwin_s, h2, qi - 1, "in_sequence") for h2 in range(2)]
              + [(win, kwin_s, vwin_s, h2, qi, "causal") for h2 in range(2)], q_augs)

    for h2 in range(2):
        for g in range(grp):
            hg = h2 * grp + g
            gate = lambda branch: gt_ref[0, branch * heads + hg:branch * heads + hg + 1, :]
            o_ref[0, hg * dh:(hg + 1) * dh, :] = (
                gate(0) * o_cmps[h2][g] + gate(1) * result(sel, h2, g) + gate(2) * result(win, h2, g)
            ).astype(o_ref.dtype)


def _nsa_attention(q_t, qr_t, gates_t, kvc, kv):
    b, d, s = q_t.shape
    tq, tk = NSA_TQ, NSA_TK
    assert WINDOW == 2 * tq and tq % SEL_BLOCK == 0 and tq % tk == 0
    dh = NSA_DH
    pairs = NSA_KV_HEADS // 2
    heads = 2 * NSA_GROUP
    pw = heads * dh
    n_cmp_pad = kvc.shape[2]
    n_sel = s // SEL_BLOCK
    wt = jnp.asarray(_cmp_to_sel_weights_t(n_cmp_pad, n_sel), BF16)
    onehot = np.zeros((s, dh), np.float32)
    onehot[np.arange(s), np.arange(s) // SEL_BLOCK] = 1.0
    kv_spec = lambda grp: pl.BlockSpec((1, s, LANE), lambda bi, p, i: (bi, 0, pairs * grp + p))
    return pl.pallas_call(
        functools.partial(_nsa_kernel, tq=tq, tk=tk),
        out_shape=jax.ShapeDtypeStruct((b, d, s), BF16),
        grid=(b, pairs, s // tq),
        in_specs=[
            pl.BlockSpec((1, pw, tq), lambda bi, p, i: (bi, p, i)),
            pl.BlockSpec((1, pw, tq), lambda bi, p, i: (bi, p, i)),
            pl.BlockSpec((1, LANE, tq), lambda bi, p, i: (bi, p, i)),
            pl.BlockSpec((1, 2, n_cmp_pad, LANE), lambda bi, p, i: (bi, p, 0, 0)),
            kv_spec(0), kv_spec(1), kv_spec(2), kv_spec(3),
            _const_spec(wt.shape),
            _const_spec(onehot.shape),
        ],
        out_specs=pl.BlockSpec((1, pw, tq), lambda bi, p, i: (bi, p, i)),
        scratch_shapes=[
            pltpu.VMEM((2, s, 2 * dh), BF16),
            pltpu.VMEM((2, s, 2 * dh), BF16),
            pltpu.VMEM((2, s // tk, NSA_ACC_ROWS, tk), BF16),
            pltpu.VMEM((2, s // tk, NSA_ACC_ROWS, tk), BF16),
            pltpu.VMEM((2, dh, n_cmp_pad), BF16),
            pltpu.VMEM((2 * heads, 1, tq), F32),
            pltpu.VMEM((2 * heads, NSA_ACC_ROWS, tq), F32),
        ],
        compiler_params=_params("parallel", "parallel", "arbitrary"),
    )(q_t, qr_t, gates_t, kvc, kv, kv, kv, kv, wt, jnp.asarray(onehot, BF16))


def _rope_tables(seq, width):
    half = NSA_DH // 2
    inv = ROPE_THETA ** (-jnp.arange(half, dtype=F32) / half)
    ang = jnp.arange(seq, dtype=F32)[:, None] * inv[None, :]
    cos, sin = jnp.cos(ang), jnp.sin(ang)
    reps = width // NSA_DH
    cos_full = jnp.tile(jnp.concatenate([cos, cos], axis=1), (1, reps))
    sin_signed = jnp.tile(jnp.concatenate([-sin, sin], axis=1), (1, reps))
    return cos_full, sin_signed


def _nsa_gate_weights(w_gate):
    d = w_gate.shape[0]
    pairs = NSA_KV_HEADS // 2
    heads = 2 * NSA_GROUP
    wg = w_gate.reshape(d, 3, pairs, heads).transpose(0, 2, 1, 3).reshape(d, pairs, 3 * heads)
    return jnp.pad(wg, ((0, 0), (0, 0), (0, LANE - 3 * heads))).reshape(d, pairs * LANE)


def kernel(x, norm_mix, norm_ffn, gla_w_in, gla_w_alpha_up, gla_b_alpha, gla_norm, gla_w_o, kv_norm, nsa_w_kv, cmp_pe_k, cmp_pe_v, cmp_k_w1, cmp_k_w2, cmp_v_w1, cmp_v_w2, nsa_w_in, nsa_w_o, ffn_w_up, ffn_conv_w, ffn_conv_b, ffn_w_down, norm_final):
    b, s, d = x.shape
    n = b * s
    x2 = x.reshape(n, d)

    n_main = 2 * GLA_HEADS * GLA_DK + 2 * GLA_HEADS * GLA_DV
    w_in = gla_w_in[0]
    n_pairs = len(GATE_PIECE_PAIRS)
    lane_pad = LANE - n_pairs * GLA_GATE_RANK
    w_low = jnp.pad(jnp.tile(w_in[:, n_main:], (1, n_pairs)), ((0, 0), (0, lane_pad))).astype(BF16)
    w_up_pieces = _split3(gla_w_alpha_up[0])
    w_up6 = jnp.pad(jnp.concatenate([w_up_pieces[iw] for _, iw in GATE_PIECE_PAIRS], axis=0),
                    ((0, lane_pad), (0, 0)))
    proj, log_a = _gla_front(x2, norm_mix[0], w_in[:, :n_main].astype(BF16), w_low, w_up6, gla_b_alpha[0])
    o = _gla_core(proj.reshape(b, s, n_main), log_a.reshape(b, s, -1), gla_norm[0])
    x2 = _proj_res(o.reshape(n, -1), gla_w_o[0].astype(BF16), x2)
    x2 = _conv_ffn(x2.reshape(b, s, d), norm_ffn[0], ffn_w_up[0].astype(BF16), ffn_conv_w[0], ffn_conv_b[0],
                   ffn_w_down[0].astype(BF16)).reshape(n, d)

    hk, dh = NSA_KV_HEADS, NSA_DH
    nq = NSA_HEADS * dh
    w_nsa = nsa_w_in[0]
    kv, k16, v16, q_t, qr_t, gates_t = _nsa_front(
        x2, kv_norm, norm_mix[1], nsa_w_kv.astype(BF16), w_nsa[:, :nq].astype(BF16),
        _nsa_gate_weights(w_nsa[:, nq:]).astype(BF16), _rope_tables(s, hk * dh), seq=s,
        q_scale=dh ** -0.5 * LOG2E)
    kvc = _compress(k16, v16, cmp_pe_k, cmp_pe_v, cmp_k_w1.astype(BF16), cmp_v_w1.astype(BF16),
                    cmp_k_w2.astype(BF16), cmp_v_w2.astype(BF16), rows_per_seq=s // CMP_STRIDE)

    o_t = _nsa_attention(q_t, qr_t, gates_t, kvc, kv.reshape(b, s, -1))
    x2 = _proj_res(o_t, nsa_w_o[0].astype(BF16), x2, transposed=True)
    return _conv_ffn(x2.reshape(b, s, d), norm_ffn[1], ffn_w_up[1].astype(BF16), ffn_conv_w[1], ffn_conv_b[1],
                     ffn_w_down[1].astype(BF16), norm_final)
```

```python
import functools
import math

import numpy as np
import jax
import jax.numpy as jnp
from jax import lax
from jax.experimental import pallas as pl
from jax.experimental.pallas import tpu as pltpu

F32 = jnp.float32
BF16 = jnp.bfloat16

EPS = 1e-6
NEG_INF = -1e30
ROPE_THETA = 10000.0
LOG2E = math.log2(math.e)

LANE = 128
BF16_SUBLANES = 16
VMEM_LIMIT_BYTES = 56 * 1024 * 1024

GLA_HEADS = 4
GLA_DK = 128
GLA_DV = 256
GLA_GATE_RANK = 16
GLA_TAU = 16.0
GLA_CHUNK = 64
GLA_LEVEL_GROUP = 4
GATE_PIECE_PAIRS = ((0, 0), (0, 1), (1, 0), (0, 2), (1, 1), (2, 0))
NSA_HEADS = 16
NSA_KV_HEADS = 4
NSA_GROUP = 4
NSA_DH = 64
CMP_BLOCK = 32
CMP_STRIDE = 16
CMP_HIDDEN = 256
SEL_BLOCK = 64
SEL_TOPK = 16
WINDOW = 512
FORCE_SCORE = 1e4
CONV_WIDTH = 3
CONV_HALO = 16

PROJ_ROWS = 1024
PROJ_COLS = 512
GLA_ROWS = 1024
GLA_CHUNKS_PER_TRIP = 4
FFN_ROWS = 1024
FFN_COLS = 256
FFN_TILES_PER_TRIP = 4
CMP_ROWS = 512
NSA_TQ = 256
NSA_TK = 256
NSA_SEL_TILES_PER_TRIP = 4
NSA_ACC_ROWS = NSA_DH + BF16_SUBLANES


def _dot(a, b):
    return jnp.dot(a, b, preferred_element_type=F32)


def _dot_nt(a, b):
    return lax.dot_general(a, b, (((1,), (1,)), ((), ())), preferred_element_type=F32)


def _dot_tn(a, b):
    return lax.dot_general(a, b, (((0,), (0,)), ((), ())), preferred_element_type=F32)


def _rms(x, g):
    return x * lax.rsqrt(jnp.mean(x * x, axis=-1, keepdims=True) + EPS) * g


def _split3(x):
    hi = x.astype(BF16)
    r1 = x - hi.astype(F32)
    mid = r1.astype(BF16)
    lo = (r1 - mid.astype(F32)).astype(BF16)
    return hi, mid, lo


def _rope_tile(x, cos, sin_signed):
    w = x.shape[-1]
    lane = lax.broadcasted_iota(jnp.int32, x.shape, 1)
    first_half = (lane & (NSA_DH - 1)) < (NSA_DH // 2)
    partner = jnp.where(first_half, pltpu.roll(x, w - NSA_DH // 2, 1), pltpu.roll(x, NSA_DH // 2, 1))
    return x * cos + partner * sin_signed


def _params(*sem):
    return pltpu.CompilerParams(dimension_semantics=sem, vmem_limit_bytes=VMEM_LIMIT_BYTES)


def _const_spec(shape):
    nd = len(shape)
    return pl.BlockSpec(shape, lambda *_: (0,) * nd, pipeline_mode=pl.Buffered(1))


def _gla_front_kernel(x_ref, g_ref, w_ref, wl_ref, wu_ref, b_ref, o_ref, la_ref):
    hn = _rms(x_ref[...], g_ref[...]).astype(BF16)
    a_rep = _dot(hn, wl_ref[...])
    hi, mid, lo = _split3(a_rep)
    group = lax.broadcasted_iota(jnp.int32, a_rep.shape, 1) // GLA_GATE_RANK
    a_piece = [g for g, (ia, _) in enumerate(GATE_PIECE_PAIRS)]
    is_piece = lambda p: functools.reduce(
        jnp.logical_or, [group == g for g in a_piece if GATE_PIECE_PAIRS[g][0] == p])
    a6 = jnp.where(is_piece(2), lo, jnp.where(is_piece(1), mid, hi))
    z = b_ref[...] + _dot(a6, wu_ref[...])
    log_sig = jnp.minimum(z, 0.0) - jnp.log(1.0 + jnp.exp(-jnp.abs(z)))
    la_ref[...] = log_sig * (1.0 / GLA_TAU)
    m = w_ref.shape[1]
    for c in range(0, m, PROJ_COLS):
        o_ref[:, c:c + PROJ_COLS] = _dot(hn, w_ref[:, c:c + PROJ_COLS]).astype(o_ref.dtype)


def _gla_front(x2, g, w_main, w_low, w_up6, b):
    n, d = x2.shape
    m = w_main.shape[1]
    hk = w_up6.shape[-1]
    tm = PROJ_ROWS
    return pl.pallas_call(
        _gla_front_kernel,
        out_shape=[jax.ShapeDtypeStruct((n, m), BF16), jax.ShapeDtypeStruct((n, hk), F32)],
        grid=(n // tm,),
        in_specs=[
            pl.BlockSpec((tm, d), lambda i: (i, 0)),
            _const_spec((1, d)),
            _const_spec((d, m)),
            _const_spec((d, LANE)),
            _const_spec((LANE, hk)),
            _const_spec((1, hk)),
        ],
        out_specs=[pl.BlockSpec((tm, m), lambda i: (i, 0)), pl.BlockSpec((tm, hk), lambda i: (i, 0))],
        compiler_params=_params("parallel"),
    )(x2, g.reshape(1, d), w_main, w_low, w_up6, b.reshape(1, hk))


def _gla_constants():
    c = GLA_CHUNK
    idx = np.arange(c)
    row, col = idx[:, None], idx[None, :]
    mats = [col <= row, col > row]
    masks = []
    h = 1
    while h < c:
        blk = idx // (2 * h)
        upper = (idx % (2 * h)) >= h
        r = (blk * 2 * h + h - 1)[:, None]
        mats.append(np.where(upper[:, None], (col > r) & (col <= row), (col > row) & (col <= r)))
        masks.append((blk[:, None] == blk[None, :]) & upper[:, None] & (~upper[None, :]))
        h *= 2
    masks.append(np.eye(c, dtype=bool))
    strips = np.zeros((len(masks), c, LANE), np.float32)
    for lvl, m in enumerate(masks):
        half = (lvl % GLA_LEVEL_GROUP) % (LANE // c)
        strips[lvl, :, half * c:(half + 1) * c] = m
    return np.concatenate(mats, 0).astype(np.float32), strips


def _gla_kernel(q_ref, k_ref, v_ref, r_ref, g_ref, gs_ref, mk_ref, gn_ref, o_ref, st_ref, *, n_chunks):
    c, dk, dv = GLA_CHUNK, GLA_DK, GLA_DV
    n_levels = mk_ref.shape[0]
    per_strip = LANE // c
    q_scale = dk ** -0.5

    @pl.when(pl.program_id(1) == 0)
    def _():
        st_ref[...] = jnp.zeros_like(st_ref)

    def decays(h, r0, t):
        ks = slice(h * dk, (h + 1) * dk)
        t["q"] = q_ref[0, pl.ds(r0, c), ks].astype(F32) * q_scale
        t["k"] = k_ref[0, pl.ds(r0, c), ks].astype(F32)
        t["v"] = v_ref[0, pl.ds(r0, c), h * dv:(h + 1) * dv]
        hi, mid, _ = _split3(g_ref[0, pl.ds(r0, c), ks])
        t["e2"] = _dot(gs_ref[...], jnp.concatenate([hi, mid], axis=1))

    def level_products(t):
        e2 = t.pop("e2")
        x = t["x"] = jnp.exp(e2[:, :dk] + e2[:, dk:])
        q, k = t["q"], t["k"]
        t["prods"] = []
        for first in range(0, n_levels, GLA_LEVEL_GROUP):
            lvls = list(range(first, min(first + GLA_LEVEL_GROUP, n_levels)))
            factor = lambda lvl: x[(2 + lvl) * c:(3 + lvl) * c] if lvl < n_levels - 1 else 1.0
            q_stack = jnp.concatenate([(q * factor(lvl)).astype(BF16) for lvl in lvls], axis=0)
            k_stack = jnp.concatenate([(k * factor(lvl)).astype(BF16) for lvl in lvls]
                                      + [jnp.zeros((c, dk), BF16)] * (GLA_LEVEL_GROUP - len(lvls)), axis=0)
            t["prods"].append((lvls, _dot_nt(q_stack, k_stack)))
        t["q_inter"] = (q * x[0:c]).astype(BF16)
        t["k_state"] = (k * x[c:2 * c]).astype(BF16)

    def outputs(h, t):
        strip = jnp.zeros((c, LANE), F32)
        for lvls, prod in t.pop("prods"):
            for j, lvl in enumerate(lvls):
                col = (j // per_strip) * LANE
                strip = strip + prod[j * c:(j + 1) * c, col:col + LANE] * mk_ref[lvl]
        att = strip[:, 0:c]
        for part in range(1, per_strip):
            att = att + strip[:, part * c:(part + 1) * c]
        st = st_ref[h]
        t["o"] = _dot(att.astype(BF16), t["v"]) + _dot_nt(t["q_inter"], st.astype(BF16))
        st_ref[h] = st * t["x"][c - 1:c] + _dot_tn(t["v"], t["k_state"])

    def finish(h, r0, t):
        o = t["o"]
        o = o * lax.rsqrt(jnp.mean(o * o, axis=-1, keepdims=True) + EPS) * gn_ref[h:h + 1, :]
        r = r_ref[0, pl.ds(r0, c), h * dv:(h + 1) * dv].astype(F32)
        o_ref[0, pl.ds(r0, c), h * dv:(h + 1) * dv] = (o * (r * jax.nn.sigmoid(r))).astype(o_ref.dtype)

    def body(ci, carry):
        items = [(h, pl.multiple_of((ci * GLA_CHUNKS_PER_TRIP + sub) * c, c), dict())
                 for sub in range(GLA_CHUNKS_PER_TRIP) for h in range(GLA_HEADS)]
        for h, r0, t in items:
            decays(h, r0, t)
        for h, r0, t in items:
            level_products(t)
        for h, r0, t in items:
            outputs(h, t)
        for h, r0, t in items:
            finish(h, r0, t)
        return carry

    lax.fori_loop(0, n_chunks // GLA_CHUNKS_PER_TRIP, body, 0)


def _gla_core(proj, log_a, gla_norm):
    b, s, _ = proj.shape
    h, dk, dv = GLA_HEADS, GLA_DK, GLA_DV
    ts = GLA_ROWS
    gs, mk = _gla_constants()
    return pl.pallas_call(
        functools.partial(_gla_kernel, n_chunks=ts // GLA_CHUNK),
        out_shape=jax.ShapeDtypeStruct((b, s, h * dv), BF16),
        grid=(b, s // ts),
        in_specs=[
            pl.BlockSpec((1, ts, h * dk), lambda i, j: (i, j, 0)),
            pl.BlockSpec((1, ts, h * dk), lambda i, j: (i, j, 1)),
            pl.BlockSpec((1, ts, h * dv), lambda i, j: (i, j, 1)),
            pl.BlockSpec((1, ts, h * dv), lambda i, j: (i, j, 2)),
            pl.BlockSpec((1, ts, h * dk), lambda i, j: (i, j, 0)),
            _const_spec(gs.shape),
            _const_spec(mk.shape),
            _const_spec((h, dv)),
        ],
        out_specs=pl.BlockSpec((1, ts, h * dv), lambda i, j: (i, j, 0)),
        scratch_shapes=[pltpu.VMEM((h, dv, dk), F32)],
        compiler_params=_params("parallel", "arbitrary"),
    )(proj, proj, proj, proj, log_a, jnp.asarray(gs, BF16), jnp.asarray(mk), gla_norm)


def _ffn_kernel(*refs, tm, final, a_transposed):
    if final:
        (rm_ref, rh_ref, am_ref, ah_ref, wo_ref, gn_ref, wup_ref, cw_ref, cb_ref, wd_ref, gf_ref,
         o_ref, hn_ref, u_ref) = refs
    else:
        (rm_ref, rh_ref, am_ref, ah_ref, wo_ref, gn_ref, wup_ref, cw_ref, cb_ref, wd_ref,
         o_ref, hn_ref, u_ref) = refs
    halo = CONV_HALO
    nf = wd_ref.shape[0]
    g = gn_ref[...]
    wo = wo_ref[...]
    half = tm // 2
    xs = []
    for j in range(2):
        rows = slice(j * half, (j + 1) * half)
        mix = _dot_tn(am_ref[0, :, rows], wo) if a_transposed else _dot(am_ref[0, rows, :], wo)
        xs.append(rm_ref[0, rows, :] + mix)
    if a_transposed:
        x_halo = rh_ref[0] + _dot_tn(ah_ref[0], wo)[ah_ref.shape[2] - halo:, :]
    else:
        x_halo = rh_ref[0] + _dot(ah_ref[0], wo)
    hn_ref[0:halo, :] = _rms(x_halo, g).astype(BF16)
    for j in range(2):
        hn_ref[halo + j * half:halo + (j + 1) * half, :] = _rms(xs[j], g).astype(BF16)
        o_ref[0, j * half:(j + 1) * half, :] = xs[j]
    keep = jnp.where(pl.program_id(1) > 0, 1.0, 0.0)

    def up_project(fi, slot):
        hn = hn_ref[...]
        for gv in range(2):
            tf = u_ref.shape[-1]
            col = pl.multiple_of((gv * nf + fi) * tf, tf)
            u = _dot(hn, wup_ref[:, pl.ds(col, tf)])
            u_ref[slot, gv] = u
            u_ref[slot, gv, 0:halo, :] = u[0:halo] * keep

    def activation(fi, slot):
        outs = []
        for gv in range(2):
            out = cb_ref[gv, fi]
            for tap in range(CONV_WIDTH):
                start = halo - (CONV_WIDTH - 1) + tap
                out = out + u_ref[slot, gv, start:start + tm, :] * cw_ref[gv, fi, tap:tap + 1, :]
            outs.append(out)
        gate, val = outs
        return (gate * jax.nn.sigmoid(gate) * val).astype(BF16)

    def run(tiles):
        up_project(tiles[0], 0)
        for j, fi in enumerate(tiles):
            if j + 1 < len(tiles):
                up_project(tiles[j + 1], (j + 1) % 2)
            o_ref[0] += _dot(activation(fi, j % 2), wd_ref[fi])

    group = FFN_TILES_PER_TRIP

    def trip(ti, carry):
        run([ti * group + j for j in range(group)])
        return carry

    lax.fori_loop(0, nf // group, trip, 0)
    if nf % group:
        run(list(range(nf - nf % group, nf)))
    if final:
        o_ref[0] = _rms(o_ref[0], gf_ref[...])


def _conv_ffn(res3, mix, w_o, g, w_up, conv_w, conv_b, w_down, final_gain=None, *, mix_transposed=False):
    b, s, d = res3.shape
    k = w_o.shape[0]
    ffn = w_down.shape[0]
    tm, tf = FFN_ROWS, FFN_COLS
    nf = ffn // tf
    halo = CONV_HALO
    final = final_gain is not None
    w_up_t = w_up
    conv_w_t = conv_w.reshape(CONV_WIDTH, 2, nf, tf).transpose(1, 2, 0, 3)
    conv_b_t = conv_b.reshape(2, nf, 1, tf)
    w_down_t = w_down.reshape(nf, tf, d)
    halo_rows = lambda bi, i: (bi, jnp.maximum(i * (tm // halo) - 1, 0), 0)
    if mix_transposed:
        mix_specs = [pl.BlockSpec((1, k, tm), lambda bi, i: (bi, 0, i)),
                     pl.BlockSpec((1, k, LANE), lambda bi, i: (bi, 0, jnp.maximum(i * (tm // LANE) - 1, 0)))]
    else:
        mix_specs = [pl.BlockSpec((1, tm, k), lambda bi, i: (bi, i, 0)), pl.BlockSpec((1, halo, k), halo_rows)]
    in_specs = [
        pl.BlockSpec((1, tm, d), lambda bi, i: (bi, i, 0)),
        pl.BlockSpec((1, halo, d), halo_rows),
        *mix_specs,
        _const_spec((k, d)),
        _const_spec((1, d)),
        _const_spec(w_up_t.shape),
        _const_spec(conv_w_t.shape),
        _const_spec(conv_b_t.shape),
        _const_spec(w_down_t.shape),
    ]
    args = [res3, res3, mix, mix, w_o, g.reshape(1, d), w_up_t, conv_w_t, conv_b_t, w_down_t]
    if final:
        in_specs.append(_const_spec((1, d)))
        args.append(final_gain.reshape(1, d))
    return pl.pallas_call(
        functools.partial(_ffn_kernel, tm=tm, final=final, a_transposed=mix_transposed),
        out_shape=jax.ShapeDtypeStruct((b, s, d), F32),
        grid=(b, s // tm),
        in_specs=in_specs,
        out_specs=pl.BlockSpec((1, tm, d), lambda bi, i: (bi, i, 0)),
        scratch_shapes=[
            pltpu.VMEM((tm + halo, d), BF16),
            pltpu.VMEM((2, 2, tm + halo, tf), F32),
        ],
        compiler_params=_params("parallel", "parallel"),
    )(*args)


def _nsa_front_kernel(x_ref, gkv_ref, gq_ref, wkv_ref, wq_ref, wg_ref, cos_ref, sin_ref,
                      kv_ref, k16_ref, v16_ref, q_ref, qr_ref, gt_ref, raw_s, *, q_scale):
    x = x_ref[...]
    y = x * lax.rsqrt(jnp.mean(x * x, axis=-1, keepdims=True) + EPS)
    hn_kv = (y * gkv_ref[...]).astype(BF16)
    hn_q = (y * gq_ref[...]).astype(BF16)
    cos, sin = cos_ref[...], sin_ref[...]
    rope_w = cos.shape[1]

    gt_ref[0] = jax.nn.sigmoid(_dot(hn_q, wg_ref[...])).T
    for grp in range(wq_ref.shape[1] // rope_w):
        acc = _dot(hn_q, wq_ref[:, grp * rope_w:(grp + 1) * rope_w]) * q_scale
        q_ref[0, grp * rope_w:(grp + 1) * rope_w, :] = acc.T.astype(q_ref.dtype)
        qr_ref[0, grp * rope_w:(grp + 1) * rope_w, :] = _rope_tile(acc, cos, sin).T.astype(qr_ref.dtype)
    n_raw = 2
    for grp in (2, 4, 3, 5):
        acc = _dot(hn_kv, wkv_ref[:, grp * rope_w:(grp + 1) * rope_w])
        if grp in (2, 4):
            acc = _rope_tile(acc, cos, sin)
        kv_ref[:, (grp - n_raw) * rope_w:(grp - n_raw + 1) * rope_w] = acc.astype(kv_ref.dtype)
    strips, groups = raw_s.shape[0], raw_s.shape[1] // CMP_STRIDE
    for grp, out_ref in ((0, k16_ref), (1, v16_ref)):
        acc = _dot(hn_kv, wkv_ref[:, grp * rope_w:(grp + 1) * rope_w])
        for j in range(strips):
            raw_s[j] = acc[:, j * LANE:(j + 1) * LANE]
        for r in range(CMP_STRIDE):
            for j in range(strips):
                out_ref[:, r * rope_w + j * LANE:r * rope_w + (j + 1) * LANE] = (
                    raw_s[j, pl.ds(r, groups, stride=CMP_STRIDE), :].astype(out_ref.dtype))


def _nsa_front(x2, g_kv, g_q, w_kv, w_q, w_g, rope, *, seq, q_scale):
    n, d = x2.shape
    tm = PROJ_ROWS
    per_seq = seq // tm
    batch = n // seq
    rope_w = rope[0].shape[1]
    row = lambda m: pl.BlockSpec((tm, m), lambda i: (i, 0))
    col = lambda m: pl.BlockSpec((1, m, tm), lambda i: (i // per_seq, 0, i % per_seq))
    n_att = w_kv.shape[1] - 2 * rope_w
    grouped = CMP_STRIDE * rope_w
    return pl.pallas_call(
        functools.partial(_nsa_front_kernel, q_scale=q_scale),
        out_shape=[
            jax.ShapeDtypeStruct((n, n_att), BF16),
            jax.ShapeDtypeStruct((n // CMP_STRIDE, grouped), BF16),
            jax.ShapeDtypeStruct((n // CMP_STRIDE, grouped), BF16),
            jax.ShapeDtypeStruct((batch, w_q.shape[1], seq), BF16),
            jax.ShapeDtypeStruct((batch, w_q.shape[1], seq), BF16),
            jax.ShapeDtypeStruct((batch, w_g.shape[1], seq), F32),
        ],
        grid=(n // tm,),
        in_specs=[
            row(d), _const_spec((1, d)), _const_spec((1, d)),
            _const_spec(w_kv.shape), _const_spec(w_q.shape), _const_spec(w_g.shape),
            pl.BlockSpec((tm, rope_w), lambda i: (i % per_seq, 0)),
            pl.BlockSpec((tm, rope_w), lambda i: (i % per_seq, 0)),
        ],
        out_specs=[row(n_att),
                   pl.BlockSpec((tm // CMP_STRIDE, grouped), lambda i: (i, 0)),
                   pl.BlockSpec((tm // CMP_STRIDE, grouped), lambda i: (i, 0)),
                   col(w_q.shape[1]), col(w_q.shape[1]), col(w_g.shape[1])],
        scratch_shapes=[pltpu.VMEM((rope_w // LANE, tm, LANE), F32)],
        compiler_params=_params("parallel"),
    )(x2, g_kv.reshape(1, d), g_q.reshape(1, d), w_kv, w_q, w_g, *rope)


def _compress_kernel(tk_ref, tv_ref, w1k_ref, w1v_ref, bk_ref, bv_ref, w2k_ref, w2v_ref, o_ref):
    rows = tk_ref.shape[0]
    dh = NSA_DH
    width = NSA_KV_HEADS * dh
    row = lax.broadcasted_iota(jnp.int32, (rows, 1), 0)
    valid = jnp.where(row < rows - 1, 1.0, 0.0)

    def first_layer(t_ref, w1_ref, h):
        y = None
        for r in range(CMP_STRIDE):
            part = _dot(t_ref[:, r * width + h * dh:r * width + (h + 1) * dh], w1_ref[r * dh:(r + 1) * dh, :])
            y = part if y is None else y + part
        return y

    def second_layer(y, b_ref, w2_ref):
        hid = y.shape[1] // 2
        pre = y[:, :hid] + pltpu.roll(y[:, hid:], rows - 1, 0) + b_ref[0:1, :]
        return _dot(jax.nn.gelu(pre).astype(BF16), w2_ref[...]) * valid

    for h in range(NSA_KV_HEADS):
        yk = first_layer(tk_ref, w1k_ref, h)
        yv = first_layer(tv_ref, w1v_ref, h)
        o_ref[0, h] = jnp.concatenate(
            [second_layer(yk, bk_ref, w2k_ref), second_layer(yv, bv_ref, w2v_ref)], axis=1).astype(o_ref.dtype)


def _pe_bias_kernel(pk_ref, pv_ref, w1k_ref, w1v_ref, ok_ref, ov_ref):
    ok_ref[...] = _dot(pk_ref[...], w1k_ref[...])
    ov_ref[...] = _dot(pv_ref[...], w1v_ref[...])


def _compress(tk, tv, pe_k, pe_v, w1k, w1v, w2k, w2v, *, rows_per_seq):
    r, width = tk.shape
    hid = w1k.shape[1]
    dh = w2k.shape[1]
    kdim = CMP_STRIDE * dh
    batch = r // rows_per_seq
    sub = 8
    pk = jnp.broadcast_to(pe_k.reshape(1, -1), (sub, 2 * kdim)).astype(BF16)
    pv = jnp.broadcast_to(pe_v.reshape(1, -1), (sub, 2 * kdim)).astype(BF16)
    bk, bv = pl.pallas_call(
        _pe_bias_kernel,
        out_shape=[jax.ShapeDtypeStruct((sub, hid), F32)] * 2,
    )(pk, pv, w1k, w1v)
    w1k_cat = jnp.concatenate([w1k[:kdim], w1k[kdim:]], axis=1)
    w1v_cat = jnp.concatenate([w1v[:kdim], w1v[kdim:]], axis=1)
    return pl.pallas_call(
        _compress_kernel,
        out_shape=jax.ShapeDtypeStruct((batch, NSA_KV_HEADS, rows_per_seq, 2 * dh), BF16),
        grid=(batch,),
        in_specs=[
            pl.BlockSpec((rows_per_seq, width), lambda i: (i, 0)),
            pl.BlockSpec((rows_per_seq, width), lambda i: (i, 0)),
            _const_spec((kdim, 2 * hid)),
            _const_spec((kdim, 2 * hid)),
            _const_spec((sub, hid)),
            _const_spec((sub, hid)),
            _const_spec((hid, dh)),
            _const_spec((hid, dh)),
        ],
        out_specs=pl.BlockSpec((1, NSA_KV_HEADS, rows_per_seq, 2 * dh), lambda i: (i, 0, 0, 0)),
        compiler_params=_params("parallel"),
    )(tk, tv, w1k_cat, w1v_cat, bk, bv, w2k, w2v)


def _cmp_to_sel_weights_t(n_cmp_pad, n_sel):
    c0 = np.arange(n_cmp_pad)[None, :] * CMP_STRIDE
    s0 = np.arange(n_sel)[:, None] * SEL_BLOCK
    ov = np.clip(np.minimum(c0 + CMP_BLOCK, s0 + SEL_BLOCK) - np.maximum(c0, s0), 0, None)
    return (ov / CMP_BLOCK).astype(np.float32)


def _nsa_kernel(q_ref, qr_ref, gt_ref, kc_ref, ks_ref, vs_ref, kw_ref, vw_ref, wt_ref, oh_ref, o_ref,
                ksel_s, kwin_s, vsel_s, vwin_s, vcmp_s, m_s, acc_s, *, tq, tk):
    grp, dh = NSA_GROUP, NSA_DH
    heads = 2 * grp
    qi = pl.program_id(2)
    t0 = qi * tq
    n_sel = wt_ref.shape[0]
    n_cmp_pad = wt_ref.shape[1]
    seq = ks_ref.shape[1]

    @pl.when(qi == 0)
    def _():
        ones_row = jnp.where(lax.broadcasted_iota(jnp.int32, (BF16_SUBLANES, tk), 0) == 0, 1.0, 0.0).astype(BF16)
        for h2 in range(2):
            cols = slice(h2 * dh, (h2 + 1) * dh)
            ksel_s[h2, :, 0:dh] = ks_ref[0, :, cols]
            ksel_s[h2, :, dh:2 * dh] = oh_ref[...]
            kwin_s[h2, :, 0:dh] = kw_ref[0, :, cols]
            kwin_s[h2, :, dh:2 * dh] = jnp.zeros((seq, dh), BF16)
            vcmp_s[h2] = kc_ref[0, h2, :, dh:2 * dh].astype(F32).T.astype(BF16)

        def build(j, carry):
            r0 = pl.multiple_of(j * tk, tk)
            for src, dst in ((vs_ref, vsel_s), (vw_ref, vwin_s)):
                blk_t = src[0, pl.ds(r0, tk), :].astype(F32).T
                for h2 in range(2):
                    dst[h2, j, 0:dh, :] = blk_t[h2 * dh:(h2 + 1) * dh].astype(BF16)
                    dst[h2, j, dh:, :] = ones_row
            return carry

        lax.fori_loop(0, seq // tk, build, 0)

    tpos = t0 + lax.broadcasted_iota(jnp.int32, (1, tq), 1)
    key_iota = lax.broadcasted_iota(jnp.int32, (tk, 1), 0)
    cmp_end = lax.broadcasted_iota(jnp.int32, (n_cmp_pad, 1), 0) * CMP_STRIDE + (CMP_BLOCK - 1)
    cmp_mask = cmp_end <= tpos
    blk = lax.broadcasted_iota(jnp.int32, (n_sel, 1), 0)
    cur = tpos >> 6
    forced = (blk == 0) | (blk == cur) | (blk == cur - 1)
    win_lo = jnp.maximum(tpos - WINDOW, -1)

    def run_tiles(tiles):
        chunks = []
        for sub in range(tq // tk):
            for branch, k_s, v_s, q_heads, h2, tile, mode in tiles:
                kt = jnp.maximum(tile, 0) * (tq // tk) + sub
                k0 = pl.multiple_of(kt * tk, tk)
                shared = dict(k=k_s[h2, pl.ds(k0, tk), :], v=v_s[h2, kt], mode=mode,
                              kpos=tile * tq + sub * tk + key_iota)
                for g in range(grp):
                    chunks.append(dict(shared, slot=(h2 * 2 + branch) * grp + g, q=q_heads[h2][g]))

        def scores(c):
            s = _dot(c["k"], c["q"])
            if c["mode"] == "causal":
                s = jnp.where(c["kpos"] <= tpos, s, NEG_INF)
            elif c["mode"] == "window_start":
                s = jnp.where(c["kpos"] > win_lo, s, NEG_INF)
            elif c["mode"] == "in_sequence":
                s = jnp.where(c["kpos"] >= 0, s, NEG_INF)
            c["s"] = s

        def softmax_update(c):
            m = m_s[c["slot"]]
            m_new = jnp.maximum(m, jnp.max(c["s"], axis=0, keepdims=True))
            c["p"] = jnp.exp2(c.pop("s") - m_new).astype(BF16)
            c["alpha"] = jnp.exp2(m - m_new)
            m_s[c["slot"]] = m_new

        def accumulate(c):
            acc_s[c["slot"]] = c["alpha"] * acc_s[c["slot"]] + _dot(c["v"], c.pop("p"))

        ahead = 6
        for c in chunks[:ahead]:
            scores(c)
        for i, c in enumerate(chunks):
            softmax_update(c)
            if i + ahead < len(chunks):
                scores(chunks[i + ahead])
            accumulate(c)

    def result(branch, h2, g):
        slot = (h2 * 2 + branch) * grp + g
        return acc_s[slot, 0:dh, :] / acc_s[slot, dh:dh + 1, :]

    m_s[...] = jnp.full(m_s.shape, NEG_INF, F32)
    acc_s[...] = jnp.zeros(acc_s.shape, F32)
    q_augs, o_cmps = [], []
    cmp_scores = [[_dot(kc_ref[0, h2, :, 0:dh], q_ref[0, (h2 * grp + g) * dh:(h2 * grp + g + 1) * dh, :])
                   for g in range(grp)] for h2 in range(2)]

    for h2 in range(2):
        head_rows = lambda ref, g: ref[0, (h2 * grp + g) * dh:(h2 * grp + g + 1) * dh, :]

        p_grp = jnp.zeros((n_cmp_pad, tq), F32)
        o_cmp = []
        for g in range(grp):
            s = jnp.where(cmp_mask, cmp_scores[h2][g], NEG_INF)
            e = jnp.where(cmp_mask, jnp.exp2(s - jnp.max(s, axis=0, keepdims=True)), 0.0)
            inv = 1.0 / jnp.maximum(jnp.sum(e, axis=0, keepdims=True), 1e-30)
            o_cmp.append(_dot(vcmp_s[h2], e.astype(BF16)) * inv)
            p_grp = p_grp + e * inv
        i3 = _dot(wt_ref[...], jnp.concatenate(_split3(p_grp), axis=1))
        imp = i3[:, 0:tq] + i3[:, tq:2 * tq] + i3[:, 2 * tq:]
        imp = jnp.where(blk > cur, -1.0, jnp.where(forced, FORCE_SCORE, imp))
        rows = 8
        ranks = []
        for r0 in range(0, n_sel, rows):
            mine = imp[r0:r0 + rows]
            row_id = blk[r0:r0 + rows]
            rank = jnp.zeros((rows, tq), F32)
            for jp in range(n_sel):
                other = imp[jp:jp + 1, :]
                ge = lambda: jnp.where(other >= mine, 1.0, 0.0)
                gt_ = lambda: jnp.where(other > mine, 1.0, 0.0)
                if jp < r0:
                    rank = rank + ge()
                elif jp >= r0 + rows:
                    rank = rank + gt_()
                else:
                    rank = rank + jnp.where(row_id > jp, ge(), gt_())
            ranks.append(rank)
        rank = jnp.concatenate(ranks, axis=0)
        sel_bias = jnp.where(rank < float(SEL_TOPK), 0.0, NEG_INF).astype(BF16)
        pad = jnp.zeros((dh - n_sel, tq), BF16)
        q_augs.append([jnp.concatenate([head_rows(qr_ref, g), sel_bias, pad], axis=0) for g in range(grp)])
        o_cmps.append(o_cmp)

    sel, win = 0, 1

    no_bias = jnp.zeros((dh, tq), BF16)
    q_wins = [[jnp.concatenate([qr_ref[0, (h2 * grp + g) * dh:(h2 * grp + g + 1) * dh, :], no_bias], axis=0)
               for g in range(grp)] for h2 in range(2)]
    run_tiles([(win, kwin_s, vwin_s, q_wins, h2, qi - 2, "window_start") for h2 in range(2)]
              + [(win, kwin_s, vwin_s, q_wins, h2, qi - 1, "in_sequence") for h2 in range(2)]
              + [(win, kwin_s, vwin_s, q_wins, h2, qi, "causal") for h2 in range(2)]
              + [(sel, ksel_s, vsel_s, q_augs, h2, qi, "causal") for h2 in range(2)])

    def sel_tiles(kts):
        return [(sel, ksel_s, vsel_s, q_augs, h2, kt, None) for kt in kts for h2 in range(2)]

    group = NSA_SEL_TILES_PER_TRIP

    def sel_body(trip, carry):
        run_tiles(sel_tiles([trip * group + j for j in range(group)]))
        return carry

    lax.fori_loop(0, qi // group, sel_body, 0)
    done = (qi // group) * group
    size = group // 2
    while size >= 1:
        @pl.when(((qi - done) & size) != 0)
        def _(done=done, size=size):
            run_tiles(sel_tiles([done + j for j in range(size)]))

        done = done + ((qi - done) & size)
        size //= 2

    for h2 in range(2):
        for g in range(grp):
            hg = h2 * grp + g
            gate = lambda branch: gt_ref[0, branch * heads + hg:branch * heads + hg + 1, :]
            o_ref[0, hg * dh:(hg + 1) * dh, :] = (
                gate(0) * o_cmps[h2][g] + gate(1) * result(sel, h2, g) + gate(2) * result(win, h2, g)
            ).astype(o_ref.dtype)


def _nsa_attention(q_t, qr_t, gates_t, kvc, kv):
    b, d, s = q_t.shape
    tq, tk = NSA_TQ, NSA_TK
    assert WINDOW == 2 * tq and tq % SEL_BLOCK == 0 and tq % tk == 0
    dh = NSA_DH
    pairs = NSA_KV_HEADS // 2
    heads = 2 * NSA_GROUP
    pw = heads * dh
    n_cmp_pad = kvc.shape[2]
    n_sel = s // SEL_BLOCK
    wt = jnp.asarray(_cmp_to_sel_weights_t(n_cmp_pad, n_sel), BF16)
    onehot = np.zeros((s, dh), np.float32)
    onehot[np.arange(s), np.arange(s) // SEL_BLOCK] = 1.0
    kv_spec = lambda grp: pl.BlockSpec((1, s, LANE), lambda bi, p, i: (bi, 0, pairs * grp + p))
    return pl.pallas_call(
        functools.partial(_nsa_kernel, tq=tq, tk=tk),
        out_shape=jax.ShapeDtypeStruct((b, d, s), BF16),
        grid=(b, pairs, s // tq),
        in_specs=[
            pl.BlockSpec((1, pw, tq), lambda bi, p, i: (bi, p, i)),
            pl.BlockSpec((1, pw, tq), lambda bi, p, i: (bi, p, i)),
            pl.BlockSpec((1, LANE, tq), lambda bi, p, i: (bi, p, i)),
            pl.BlockSpec((1, 2, n_cmp_pad, LANE), lambda bi, p, i: (bi, p, 0, 0)),
            kv_spec(0), kv_spec(1), kv_spec(2), kv_spec(3),
            _const_spec(wt.shape),
            _const_spec(onehot.shape),
        ],
        out_specs=pl.BlockSpec((1, pw, tq), lambda bi, p, i: (bi, p, i)),
        scratch_shapes=[
            pltpu.VMEM((2, s, 2 * dh), BF16),
            pltpu.VMEM((2, s, 2 * dh), BF16),
            pltpu.VMEM((2, s // tk, NSA_ACC_ROWS, tk), BF16),
            pltpu.VMEM((2, s // tk, NSA_ACC_ROWS, tk), BF16),
            pltpu.VMEM((2, dh, n_cmp_pad), BF16),
            pltpu.VMEM((2 * heads, 1, tq), F32),
            pltpu.VMEM((2 * heads, NSA_ACC_ROWS, tq), F32),
        ],
        compiler_params=_params("parallel", "parallel", "arbitrary"),
    )(q_t, qr_t, gates_t, kvc, kv, kv, kv, kv, wt, jnp.asarray(onehot, BF16))


def _rope_tables(seq, width):
    half = NSA_DH // 2
    inv = ROPE_THETA ** (-jnp.arange(half, dtype=F32) / half)
    ang = jnp.arange(seq, dtype=F32)[:, None] * inv[None, :]
    cos, sin = jnp.cos(ang), jnp.sin(ang)
    reps = width // NSA_DH
    cos_full = jnp.tile(jnp.concatenate([cos, cos], axis=1), (1, reps))
    sin_signed = jnp.tile(jnp.concatenate([-sin, sin], axis=1), (1, reps))
    return cos_full, sin_signed


def _nsa_gate_weights(w_gate):
    d = w_gate.shape[0]
    pairs = NSA_KV_HEADS // 2
    heads = 2 * NSA_GROUP
    wg = w_gate.reshape(d, 3, pairs, heads).transpose(0, 2, 1, 3).reshape(d, pairs, 3 * heads)
    return jnp.pad(wg, ((0, 0), (0, 0), (0, LANE - 3 * heads))).reshape(d, pairs * LANE)


def kernel(x, norm_mix, norm_ffn, gla_w_in, gla_w_alpha_up, gla_b_alpha, gla_norm, gla_w_o, kv_norm, nsa_w_kv, cmp_pe_k, cmp_pe_v, cmp_k_w1, cmp_k_w2, cmp_v_w1, cmp_v_w2, nsa_w_in, nsa_w_o, ffn_w_up, ffn_conv_w, ffn_conv_b, ffn_w_down, norm_final):
    b, s, d = x.shape
    n = b * s
    x2 = x.reshape(n, d)

    n_main = 2 * GLA_HEADS * GLA_DK + 2 * GLA_HEADS * GLA_DV
    w_in = gla_w_in[0]
    n_pairs = len(GATE_PIECE_PAIRS)
    lane_pad = LANE - n_pairs * GLA_GATE_RANK
    w_low = jnp.pad(jnp.tile(w_in[:, n_main:], (1, n_pairs)), ((0, 0), (0, lane_pad))).astype(BF16)
    w_up_pieces = _split3(gla_w_alpha_up[0])
    w_up6 = jnp.pad(jnp.concatenate([w_up_pieces[iw] for _, iw in GATE_PIECE_PAIRS], axis=0),
                    ((0, lane_pad), (0, 0)))
    proj, log_a = _gla_front(x2, norm_mix[0], w_in[:, :n_main].astype(BF16), w_low, w_up6, gla_b_alpha[0])
    o = _gla_core(proj.reshape(b, s, n_main), log_a.reshape(b, s, -1), gla_norm[0])
    x2 = _conv_ffn(x, o, gla_w_o[0].astype(BF16), norm_ffn[0], ffn_w_up[0].astype(BF16), ffn_conv_w[0],
                   ffn_conv_b[0], ffn_w_down[0].astype(BF16)).reshape(n, d)

    hk, dh = NSA_KV_HEADS, NSA_DH
    nq = NSA_HEADS * dh
    w_nsa = nsa_w_in[0]
    kv, k16, v16, q_t, qr_t, gates_t = _nsa_front(
        x2, kv_norm, norm_mix[1], nsa_w_kv.astype(BF16), w_nsa[:, :nq].astype(BF16),
        _nsa_gate_weights(w_nsa[:, nq:]).astype(BF16), _rope_tables(s, hk * dh), seq=s,
        q_scale=dh ** -0.5 * LOG2E)
    kvc = _compress(k16, v16, cmp_pe_k, cmp_pe_v, cmp_k_w1.astype(BF16), cmp_v_w1.astype(BF16),
                    cmp_k_w2.astype(BF16), cmp_v_w2.astype(BF16), rows_per_seq=s // CMP_STRIDE)

    o_t = _nsa_attention(q_t, qr_t, gates_t, kvc, kv.reshape(b, s, -1))
    return _conv_ffn(x2.reshape(b, s, d), o_t, nsa_w_o[0].astype(BF16), norm_ffn[1], ffn_w_up[1].astype(BF16),
                     ffn_conv_w[1], ffn_conv_b[1], ffn_w_down[1].astype(BF16), norm_final, mix_transposed=True)
```

```python
import functools
import math

import numpy as np
import jax
import jax.numpy as jnp
from jax import lax
from jax.experimental import pallas as pl
from jax.experimental.pallas import tpu as pltpu

F32 = jnp.float32
BF16 = jnp.bfloat16

EPS = 1e-6
NEG_INF = -1e30
ROPE_THETA = 10000.0
LOG2E = math.log2(math.e)

LANE = 128
BF16_SUBLANES = 16
VMEM_LIMIT_BYTES = 56 * 1024 * 1024

GLA_HEADS = 4
GLA_DK = 128
GLA_DV = 256
GLA_GATE_RANK = 16
GLA_TAU = 16.0
GLA_CHUNK = 64
GLA_LEVEL_GROUP = 4
GATE_PIECE_PAIRS = ((0, 0), (0, 1), (1, 0), (0, 2), (1, 1), (2, 0))
NSA_HEADS = 16
NSA_KV_HEADS = 4
NSA_GROUP = 4
NSA_DH = 64
CMP_BLOCK = 32
CMP_STRIDE = 16
CMP_HIDDEN = 256
SEL_BLOCK = 64
SEL_TOPK = 16
WINDOW = 512
FORCE_SCORE = 1e4
CONV_WIDTH = 3
CONV_HALO = 16

PROJ_ROWS = 1024
PROJ_COLS = 512
GLA_ROWS = 1024
GLA_CHUNKS_PER_TRIP = 8
FFN_ROWS = 1024
FFN_COLS = 256
FFN_TILES_PER_TRIP = 4
CMP_ROWS = 512
NSA_TQ = 256
NSA_TK = 256
NSA_SEL_TILES_PER_TRIP = 4
NSA_SCORES_AHEAD = 6
NSA_ACC_ROWS = NSA_DH + BF16_SUBLANES


def _dot(a, b):
    return jnp.dot(a, b, preferred_element_type=F32)


def _dot_nt(a, b):
    return lax.dot_general(a, b, (((1,), (1,)), ((), ())), preferred_element_type=F32)


def _dot_tn(a, b):
    return lax.dot_general(a, b, (((0,), (0,)), ((), ())), preferred_element_type=F32)


def _rms(x, g):
    return x * lax.rsqrt(jnp.mean(x * x, axis=-1, keepdims=True) + EPS) * g


def _split3(x):
    hi = x.astype(BF16)
    r1 = x - hi.astype(F32)
    mid = r1.astype(BF16)
    lo = (r1 - mid.astype(F32)).astype(BF16)
    return hi, mid, lo


def _rope_tile(x, cos, sin_signed):
    w = x.shape[-1]
    lane = lax.broadcasted_iota(jnp.int32, x.shape, 1)
    first_half = (lane & (NSA_DH - 1)) < (NSA_DH // 2)
    partner = jnp.where(first_half, pltpu.roll(x, w - NSA_DH // 2, 1), pltpu.roll(x, NSA_DH // 2, 1))
    return x * cos + partner * sin_signed


def _params(*sem):
    return pltpu.CompilerParams(dimension_semantics=sem, vmem_limit_bytes=VMEM_LIMIT_BYTES)


def _const_spec(shape):
    nd = len(shape)
    return pl.BlockSpec(shape, lambda *_: (0,) * nd, pipeline_mode=pl.Buffered(1))


def _gla_front_kernel(x_ref, g_ref, w_ref, wl_ref, wu_ref, b_ref, o_ref, la_ref):
    hn = _rms(x_ref[...], g_ref[...]).astype(BF16)
    a_rep = _dot(hn, wl_ref[...])
    hi, mid, lo = _split3(a_rep)
    group = lax.broadcasted_iota(jnp.int32, a_rep.shape, 1) // GLA_GATE_RANK
    a_piece = [g for g, (ia, _) in enumerate(GATE_PIECE_PAIRS)]
    is_piece = lambda p: functools.reduce(
        jnp.logical_or, [group == g for g in a_piece if GATE_PIECE_PAIRS[g][0] == p])
    a6 = jnp.where(is_piece(2), lo, jnp.where(is_piece(1), mid, hi))
    z = b_ref[...] + _dot(a6, wu_ref[...])
    log_sig = jnp.minimum(z, 0.0) - jnp.log(1.0 + jnp.exp(-jnp.abs(z)))
    la_ref[...] = log_sig * (1.0 / GLA_TAU)
    m = w_ref.shape[1]
    for c in range(0, m, PROJ_COLS):
        o_ref[:, c:c + PROJ_COLS] = _dot(hn, w_ref[:, c:c + PROJ_COLS]).astype(o_ref.dtype)


def _gla_front(x2, g, w_main, w_low, w_up6, b):
    n, d = x2.shape
    m = w_main.shape[1]
    hk = w_up6.shape[-1]
    tm = PROJ_ROWS
    return pl.pallas_call(
        _gla_front_kernel,
        out_shape=[jax.ShapeDtypeStruct((n, m), BF16), jax.ShapeDtypeStruct((n, hk), F32)],
        grid=(n // tm,),
        in_specs=[
            pl.BlockSpec((tm, d), lambda i: (i, 0)),
            _const_spec((1, d)),
            _const_spec((d, m)),
            _const_spec((d, LANE)),
            _const_spec((LANE, hk)),
            _const_spec((1, hk)),
        ],
        out_specs=[pl.BlockSpec((tm, m), lambda i: (i, 0)), pl.BlockSpec((tm, hk), lambda i: (i, 0))],
        compiler_params=_params("parallel"),
    )(x2, g.reshape(1, d), w_main, w_low, w_up6, b.reshape(1, hk))


def _gla_constants():
    c = GLA_CHUNK
    idx = np.arange(c)
    row, col = idx[:, None], idx[None, :]
    mats = [col <= row, col > row]
    masks = []
    h = 1
    while h < c:
        blk = idx // (2 * h)
        upper = (idx % (2 * h)) >= h
        r = (blk * 2 * h + h - 1)[:, None]
        mats.append(np.where(upper[:, None], (col > r) & (col <= row), (col > row) & (col <= r)))
        masks.append((blk[:, None] == blk[None, :]) & upper[:, None] & (~upper[None, :]))
        h *= 2
    masks.append(np.eye(c, dtype=bool))
    strips = np.zeros((len(masks), c, LANE), np.float32)
    for lvl, m in enumerate(masks):
        half = (lvl % GLA_LEVEL_GROUP) % (LANE // c)
        strips[lvl, :, half * c:(half + 1) * c] = m
    return np.concatenate(mats, 0).astype(np.float32), strips


def _gla_kernel(q_ref, k_ref, v_ref, r_ref, g_ref, gs_ref, mk_ref, gn_ref, o_ref, st_ref, *, n_chunks):
    c, dk, dv = GLA_CHUNK, GLA_DK, GLA_DV
    n_levels = mk_ref.shape[0]
    per_strip = LANE // c
    q_scale = dk ** -0.5

    @pl.when(pl.program_id(1) == 0)
    def _():
        st_ref[...] = jnp.zeros_like(st_ref)

    def decays(h, r0, t):
        ks = slice(h * dk, (h + 1) * dk)
        t["q"] = q_ref[0, pl.ds(r0, c), ks].astype(F32) * q_scale
        t["k"] = k_ref[0, pl.ds(r0, c), ks].astype(F32)
        t["v"] = v_ref[0, pl.ds(r0, c), h * dv:(h + 1) * dv]
        hi, mid, _ = _split3(g_ref[0, pl.ds(r0, c), ks])
        t["e2"] = _dot(gs_ref[...], jnp.concatenate([hi, mid], axis=1))

    def level_products(t):
        e2 = t.pop("e2")
        x = t["x"] = jnp.exp(e2[:, :dk] + e2[:, dk:])
        q, k = t["q"], t["k"]
        t["prods"] = []
        for first in range(0, n_levels, GLA_LEVEL_GROUP):
            lvls = list(range(first, min(first + GLA_LEVEL_GROUP, n_levels)))
            factor = lambda lvl: x[(2 + lvl) * c:(3 + lvl) * c] if lvl < n_levels - 1 else 1.0
            q_stack = jnp.concatenate([(q * factor(lvl)).astype(BF16) for lvl in lvls], axis=0)
            k_stack = jnp.concatenate([(k * factor(lvl)).astype(BF16) for lvl in lvls]
                                      + [jnp.zeros((c, dk), BF16)] * (GLA_LEVEL_GROUP - len(lvls)), axis=0)
            t["prods"].append((lvls, _dot_nt(q_stack, k_stack)))
        t["q_inter"] = (q * x[0:c]).astype(BF16)
        t["k_state"] = (k * x[c:2 * c]).astype(BF16)

    def outputs(h, t):
        strip = jnp.zeros((c, LANE), F32)
        for lvls, prod in t.pop("prods"):
            for j, lvl in enumerate(lvls):
                col = (j // per_strip) * LANE
                strip = strip + prod[j * c:(j + 1) * c, col:col + LANE] * mk_ref[lvl]
        att = strip[:, 0:c]
        for part in range(1, per_strip):
            att = att + strip[:, part * c:(part + 1) * c]
        st = st_ref[h]
        t["o"] = _dot(att.astype(BF16), t["v"]) + _dot_nt(t["q_inter"], st.astype(BF16))
        st_ref[h] = st * t["x"][c - 1:c] + _dot_tn(t["v"], t["k_state"])

    def finish(h, r0, t):
        o = t["o"]
        o = o * lax.rsqrt(jnp.mean(o * o, axis=-1, keepdims=True) + EPS) * gn_ref[h:h + 1, :]
        r = r_ref[0, pl.ds(r0, c), h * dv:(h + 1) * dv].astype(F32)
        o_ref[0, pl.ds(r0, c), h * dv:(h + 1) * dv] = (o * (r * jax.nn.sigmoid(r))).astype(o_ref.dtype)

    def body(ci, carry):
        items = [(h, pl.multiple_of((ci * GLA_CHUNKS_PER_TRIP + sub) * c, c), dict())
                 for sub in range(GLA_CHUNKS_PER_TRIP) for h in range(GLA_HEADS)]
        for h, r0, t in items:
            decays(h, r0, t)
        for h, r0, t in items:
            level_products(t)
        for h, r0, t in items:
            outputs(h, t)
        for h, r0, t in items:
            finish(h, r0, t)
        return carry

    lax.fori_loop(0, n_chunks // GLA_CHUNKS_PER_TRIP, body, 0)


def _gla_core(proj, log_a, gla_norm):
    b, s, _ = proj.shape
    h, dk, dv = GLA_HEADS, GLA_DK, GLA_DV
    ts = GLA_ROWS
    gs, mk = _gla_constants()
    return pl.pallas_call(
        functools.partial(_gla_kernel, n_chunks=ts // GLA_CHUNK),
        out_shape=jax.ShapeDtypeStruct((b, s, h * dv), BF16),
        grid=(b, s // ts),
        in_specs=[
            pl.BlockSpec((1, ts, h * dk), lambda i, j: (i, j, 0)),
            pl.BlockSpec((1, ts, h * dk), lambda i, j: (i, j, 1)),
            pl.BlockSpec((1, ts, h * dv), lambda i, j: (i, j, 1)),
            pl.BlockSpec((1, ts, h * dv), lambda i, j: (i, j, 2)),
            pl.BlockSpec((1, ts, h * dk), lambda i, j: (i, j, 0)),
            _const_spec(gs.shape),
            _const_spec(mk.shape),
            _const_spec((h, dv)),
        ],
        out_specs=pl.BlockSpec((1, ts, h * dv), lambda i, j: (i, j, 0)),
        scratch_shapes=[pltpu.VMEM((h, dv, dk), F32)],
        compiler_params=_params("parallel", "arbitrary"),
    )(proj, proj, proj, proj, log_a, jnp.asarray(gs, BF16), jnp.asarray(mk), gla_norm)


def _ffn_kernel(*refs, tm, final, a_transposed):
    if final:
        (rm_ref, rh_ref, am_ref, ah_ref, wo_ref, gn_ref, wup_ref, cw_ref, cb_ref, wd_ref, gf_ref,
         o_ref, hn_ref, u_ref) = refs
    else:
        (rm_ref, rh_ref, am_ref, ah_ref, wo_ref, gn_ref, wup_ref, cw_ref, cb_ref, wd_ref,
         o_ref, hn_ref, u_ref) = refs
    halo = CONV_HALO
    nf = wd_ref.shape[0]
    g = gn_ref[...]
    wo = wo_ref[...]
    half = tm // 2
    xs = []
    for j in range(2):
        rows = slice(j * half, (j + 1) * half)
        mix = _dot_tn(am_ref[0, :, rows], wo) if a_transposed else _dot(am_ref[0, rows, :], wo)
        xs.append(rm_ref[0, rows, :] + mix)
    if a_transposed:
        x_halo = rh_ref[0] + _dot_tn(ah_ref[0], wo)[ah_ref.shape[2] - halo:, :]
    else:
        x_halo = rh_ref[0] + _dot(ah_ref[0], wo)
    hn_ref[0:halo, :] = _rms(x_halo, g).astype(BF16)
    for j in range(2):
        hn_ref[halo + j * half:halo + (j + 1) * half, :] = _rms(xs[j], g).astype(BF16)
        o_ref[0, j * half:(j + 1) * half, :] = xs[j]
    keep = jnp.where(pl.program_id(1) > 0, 1.0, 0.0)

    def up_project(fi, slot):
        hn = hn_ref[...]
        for gv in range(2):
            tf = u_ref.shape[-1]
            col = pl.multiple_of((gv * nf + fi) * tf, tf)
            u = _dot(hn, wup_ref[:, pl.ds(col, tf)])
            u_ref[slot, gv] = u
            u_ref[slot, gv, 0:halo, :] = u[0:halo] * keep

    def activation(fi, slot):
        outs = []
        for gv in range(2):
            out = cb_ref[gv, fi]
            for tap in range(CONV_WIDTH):
                start = halo - (CONV_WIDTH - 1) + tap
                out = out + u_ref[slot, gv, start:start + tm, :] * cw_ref[gv, fi, tap:tap + 1, :]
            outs.append(out)
        gate, val = outs
        return (gate * jax.nn.sigmoid(gate) * val).astype(BF16)

    def run(tiles):
        up_project(tiles[0], 0)
        for j, fi in enumerate(tiles):
            if j + 1 < len(tiles):
                up_project(tiles[j + 1], (j + 1) % 2)
            o_ref[0] += _dot(activation(fi, j % 2), wd_ref[fi])

    group = FFN_TILES_PER_TRIP

    def trip(ti, carry):
        run([ti * group + j for j in range(group)])
        return carry

    lax.fori_loop(0, nf // group, trip, 0)
    if nf % group:
        run(list(range(nf - nf % group, nf)))
    if final:
        o_ref[0] = _rms(o_ref[0], gf_ref[...])


def _conv_ffn(res3, mix, w_o, g, w_up, conv_w, conv_b, w_down, final_gain=None, *, mix_transposed=False):
    b, s, d = res3.shape
    k = w_o.shape[0]
    ffn = w_down.shape[0]
    tm, tf = FFN_ROWS, FFN_COLS
    nf = ffn // tf
    halo = CONV_HALO
    final = final_gain is not None
    w_up_t = w_up
    conv_w_t = conv_w.reshape(CONV_WIDTH, 2, nf, tf).transpose(1, 2, 0, 3)
    conv_b_t = conv_b.reshape(2, nf, 1, tf)
    w_down_t = w_down.reshape(nf, tf, d)
    halo_rows = lambda bi, i: (bi, jnp.maximum(i * (tm // halo) - 1, 0), 0)
    if mix_transposed:
        mix_specs = [pl.BlockSpec((1, k, tm), lambda bi, i: (bi, 0, i)),
                     pl.BlockSpec((1, k, LANE), lambda bi, i: (bi, 0, jnp.maximum(i * (tm // LANE) - 1, 0)))]
    else:
        mix_specs = [pl.BlockSpec((1, tm, k), lambda bi, i: (bi, i, 0)), pl.BlockSpec((1, halo, k), halo_rows)]
    in_specs = [
        pl.BlockSpec((1, tm, d), lambda bi, i: (bi, i, 0)),
        pl.BlockSpec((1, halo, d), halo_rows),
        *mix_specs,
        _const_spec((k, d)),
        _const_spec((1, d)),
        _const_spec(w_up_t.shape),
        _const_spec(conv_w_t.shape),
        _const_spec(conv_b_t.shape),
        _const_spec(w_down_t.shape),
    ]
    args = [res3, res3, mix, mix, w_o, g.reshape(1, d), w_up_t, conv_w_t, conv_b_t, w_down_t]
    if final:
        in_specs.append(_const_spec((1, d)))
        args.append(final_gain.reshape(1, d))
    return pl.pallas_call(
        functools.partial(_ffn_kernel, tm=tm, final=final, a_transposed=mix_transposed),
        out_shape=jax.ShapeDtypeStruct((b, s, d), F32),
        grid=(b, s // tm),
        in_specs=in_specs,
        out_specs=pl.BlockSpec((1, tm, d), lambda bi, i: (bi, i, 0)),
        scratch_shapes=[
            pltpu.VMEM((tm + halo, d), BF16),
            pltpu.VMEM((2, 2, tm + halo, tf), F32),
        ],
        compiler_params=_params("parallel", "parallel"),
    )(*args)


def _nsa_front_kernel(x_ref, gkv_ref, gq_ref, wkv_ref, wq_ref, wg_ref, cos_ref, sin_ref,
                      kv_ref, k16_ref, v16_ref, q_ref, qr_ref, gt_ref, raw_s, *, q_scale):
    x = x_ref[...]
    y = x * lax.rsqrt(jnp.mean(x * x, axis=-1, keepdims=True) + EPS)
    hn_kv = (y * gkv_ref[...]).astype(BF16)
    hn_q = (y * gq_ref[...]).astype(BF16)
    cos, sin = cos_ref[...], sin_ref[...]
    rope_w = cos.shape[1]

    gt_ref[0] = jax.nn.sigmoid(_dot(hn_q, wg_ref[...])).T
    for grp in range(wq_ref.shape[1] // rope_w):
        acc = _dot(hn_q, wq_ref[:, grp * rope_w:(grp + 1) * rope_w]) * q_scale
        q_ref[0, grp * rope_w:(grp + 1) * rope_w, :] = acc.T.astype(q_ref.dtype)
        qr_ref[0, grp * rope_w:(grp + 1) * rope_w, :] = _rope_tile(acc, cos, sin).T.astype(qr_ref.dtype)
    n_raw, strips, groups = raw_s.shape[0], raw_s.shape[1], raw_s.shape[2] // CMP_STRIDE
    for grp, out_ref in ((0, k16_ref), (1, v16_ref)):
        acc = _dot(hn_kv, wkv_ref[:, grp * rope_w:(grp + 1) * rope_w])
        for j in range(strips):
            raw_s[grp, j] = acc[:, j * LANE:(j + 1) * LANE]
        for r in range(CMP_STRIDE):
            for j in range(strips):
                out_ref[:, r * rope_w + j * LANE:r * rope_w + (j + 1) * LANE] = (
                    raw_s[grp, j, pl.ds(r, groups, stride=CMP_STRIDE), :].astype(out_ref.dtype))
    for grp in (2, 4, 3, 5):
        acc = _dot(hn_kv, wkv_ref[:, grp * rope_w:(grp + 1) * rope_w])
        if grp in (2, 4):
            acc = _rope_tile(acc, cos, sin)
        kv_ref[:, (grp - n_raw) * rope_w:(grp - n_raw + 1) * rope_w] = acc.astype(kv_ref.dtype)


def _nsa_front(x2, g_kv, g_q, w_kv, w_q, w_g, rope, *, seq, q_scale):
    n, d = x2.shape
    tm = PROJ_ROWS
    per_seq = seq // tm
    batch = n // seq
    rope_w = rope[0].shape[1]
    row = lambda m: pl.BlockSpec((tm, m), lambda i: (i, 0))
    col = lambda m: pl.BlockSpec((1, m, tm), lambda i: (i // per_seq, 0, i % per_seq))
    n_att = w_kv.shape[1] - 2 * rope_w
    grouped = CMP_STRIDE * rope_w
    return pl.pallas_call(
        functools.partial(_nsa_front_kernel, q_scale=q_scale),
        out_shape=[
            jax.ShapeDtypeStruct((n, n_att), BF16),
            jax.ShapeDtypeStruct((n // CMP_STRIDE, grouped), BF16),
            jax.ShapeDtypeStruct((n // CMP_STRIDE, grouped), BF16),
            jax.ShapeDtypeStruct((batch, w_q.shape[1], seq), BF16),
            jax.ShapeDtypeStruct((batch, w_q.shape[1], seq), BF16),
            jax.ShapeDtypeStruct((batch, w_g.shape[1], seq), F32),
        ],
        grid=(n // tm,),
        in_specs=[
            row(d), _const_spec((1, d)), _const_spec((1, d)),
            _const_spec(w_kv.shape), _const_spec(w_q.shape), _const_spec(w_g.shape),
            pl.BlockSpec((tm, rope_w), lambda i: (i % per_seq, 0)),
            pl.BlockSpec((tm, rope_w), lambda i: (i % per_seq, 0)),
        ],
        out_specs=[row(n_att),
                   pl.BlockSpec((tm // CMP_STRIDE, grouped), lambda i: (i, 0)),
                   pl.BlockSpec((tm // CMP_STRIDE, grouped), lambda i: (i, 0)),
                   col(w_q.shape[1]), col(w_q.shape[1]), col(w_g.shape[1])],
        scratch_shapes=[pltpu.VMEM((2, rope_w // LANE, tm, LANE), F32)],
        compiler_params=_params("parallel"),
    )(x2, g_kv.reshape(1, d), g_q.reshape(1, d), w_kv, w_q, w_g, *rope)


def _compress_kernel(tk_ref, tv_ref, w1k_ref, w1v_ref, bk_ref, bv_ref, w2k_ref, w2v_ref, o_ref):
    rows = tk_ref.shape[0]
    dh = NSA_DH
    width = NSA_KV_HEADS * dh
    row = lax.broadcasted_iota(jnp.int32, (rows, 1), 0)
    valid = jnp.where(row < rows - 1, 1.0, 0.0)

    def first_layer(t_ref, w1_ref, h):
        y = None
        for r in range(CMP_STRIDE):
            part = _dot(t_ref[:, r * width + h * dh:r * width + (h + 1) * dh], w1_ref[r * dh:(r + 1) * dh, :])
            y = part if y is None else y + part
        return y

    def second_layer(y, b_ref, w2_ref):
        hid = y.shape[1] // 2
        pre = y[:, :hid] + pltpu.roll(y[:, hid:], rows - 1, 0) + b_ref[0:1, :]
        return _dot(jax.nn.gelu(pre).astype(BF16), w2_ref[...]) * valid

    for h in range(NSA_KV_HEADS):
        yk = first_layer(tk_ref, w1k_ref, h)
        yv = first_layer(tv_ref, w1v_ref, h)
        o_ref[0, h] = jnp.concatenate(
            [second_layer(yk, bk_ref, w2k_ref), second_layer(yv, bv_ref, w2v_ref)], axis=1).astype(o_ref.dtype)


def _pe_bias_kernel(pk_ref, pv_ref, w1k_ref, w1v_ref, ok_ref, ov_ref):
    ok_ref[...] = _dot(pk_ref[...], w1k_ref[...])
    ov_ref[...] = _dot(pv_ref[...], w1v_ref[...])


def _compress(tk, tv, pe_k, pe_v, w1k, w1v, w2k, w2v, *, rows_per_seq):
    r, width = tk.shape
    hid = w1k.shape[1]
    dh = w2k.shape[1]
    kdim = CMP_STRIDE * dh
    batch = r // rows_per_seq
    sub = 8
    pk = jnp.broadcast_to(pe_k.reshape(1, -1), (sub, 2 * kdim)).astype(BF16)
    pv = jnp.broadcast_to(pe_v.reshape(1, -1), (sub, 2 * kdim)).astype(BF16)
    bk, bv = pl.pallas_call(
        _pe_bias_kernel,
        out_shape=[jax.ShapeDtypeStruct((sub, hid), F32)] * 2,
    )(pk, pv, w1k, w1v)
    w1k_cat = jnp.concatenate([w1k[:kdim], w1k[kdim:]], axis=1)
    w1v_cat = jnp.concatenate([w1v[:kdim], w1v[kdim:]], axis=1)
    return pl.pallas_call(
        _compress_kernel,
        out_shape=jax.ShapeDtypeStruct((batch, NSA_KV_HEADS, rows_per_seq, 2 * dh), BF16),
        grid=(batch,),
        in_specs=[
            pl.BlockSpec((rows_per_seq, width), lambda i: (i, 0)),
            pl.BlockSpec((rows_per_seq, width), lambda i: (i, 0)),
            _const_spec((kdim, 2 * hid)),
            _const_spec((kdim, 2 * hid)),
            _const_spec((sub, hid)),
            _const_spec((sub, hid)),
            _const_spec((hid, dh)),
            _const_spec((hid, dh)),
        ],
        out_specs=pl.BlockSpec((1, NSA_KV_HEADS, rows_per_seq, 2 * dh), lambda i: (i, 0, 0, 0)),
        compiler_params=_params("parallel"),
    )(tk, tv, w1k_cat, w1v_cat, bk, bv, w2k, w2v)


def _cmp_to_sel_weights_t(n_cmp_pad, n_sel):
    c0 = np.arange(n_cmp_pad)[None, :] * CMP_STRIDE
    s0 = np.arange(n_sel)[:, None] * SEL_BLOCK
    ov = np.clip(np.minimum(c0 + CMP_BLOCK, s0 + SEL_BLOCK) - np.maximum(c0, s0), 0, None)
    return (ov / CMP_BLOCK).astype(np.float32)


def _nsa_kernel(q_ref, qr_ref, gt_ref, kc_ref, ks_ref, vs_ref, kw_ref, vw_ref, wt_ref, oh_ref, o_ref,
                ksel_s, kwin_s, vsel_s, vwin_s, vcmp_s, m_s, acc_s, *, tq, tk):
    grp, dh = NSA_GROUP, NSA_DH
    heads = 2 * grp
    qi = pl.program_id(2)
    t0 = qi * tq
    n_sel = wt_ref.shape[0]
    n_cmp_pad = wt_ref.shape[1]
    seq = ks_ref.shape[1]

    @pl.when(qi == 0)
    def _():
        ones_row = jnp.where(lax.broadcasted_iota(jnp.int32, (BF16_SUBLANES, tk), 0) == 0, 1.0, 0.0).astype(BF16)
        for h2 in range(2):
            cols = slice(h2 * dh, (h2 + 1) * dh)
            ksel_s[h2, :, 0:dh] = ks_ref[0, :, cols]
            ksel_s[h2, :, dh:2 * dh] = oh_ref[...]
            kwin_s[h2, :, 0:dh] = kw_ref[0, :, cols]
            kwin_s[h2, :, dh:2 * dh] = jnp.zeros((seq, dh), BF16)
            vcmp_s[h2] = kc_ref[0, h2, :, dh:2 * dh].astype(F32).T.astype(BF16)

        def build(j, carry):
            r0 = pl.multiple_of(j * tk, tk)
            for src, dst in ((vs_ref, vsel_s), (vw_ref, vwin_s)):
                blk_t = src[0, pl.ds(r0, tk), :].astype(F32).T
                for h2 in range(2):
                    dst[h2, j, 0:dh, :] = blk_t[h2 * dh:(h2 + 1) * dh].astype(BF16)
                    dst[h2, j, dh:, :] = ones_row
            return carry

        lax.fori_loop(0, seq // tk, build, 0)

    tpos = t0 + lax.broadcasted_iota(jnp.int32, (1, tq), 1)
    key_iota = lax.broadcasted_iota(jnp.int32, (tk, 1), 0)
    cmp_end = lax.broadcasted_iota(jnp.int32, (n_cmp_pad, 1), 0) * CMP_STRIDE + (CMP_BLOCK - 1)
    cmp_mask = cmp_end <= tpos
    blk = lax.broadcasted_iota(jnp.int32, (n_sel, 1), 0)
    cur = tpos >> 6
    forced = (blk == 0) | (blk == cur) | (blk == cur - 1)
    win_lo = jnp.maximum(tpos - WINDOW, -1)

    def run_tiles(tiles, ahead=NSA_SCORES_AHEAD):
        chunks = []
        for sub in range(tq // tk):
            for branch, k_s, v_s, q_heads, h2, tile, mode in tiles:
                kt = jnp.maximum(tile, 0) * (tq // tk) + sub
                k0 = pl.multiple_of(kt * tk, tk)
                shared = dict(k=k_s[h2, pl.ds(k0, tk), :], v=v_s[h2, kt], mode=mode,
                              kpos=tile * tq + sub * tk + key_iota)
                for g in range(grp):
                    chunks.append(dict(shared, slot=(h2 * 2 + branch) * grp + g, q=q_heads[h2][g]))

        def scores(c):
            s = _dot(c["k"], c["q"])
            if c["mode"] == "causal":
                s = jnp.where(c["kpos"] <= tpos, s, NEG_INF)
            elif c["mode"] == "window_start":
                s = jnp.where(c["kpos"] > win_lo, s, NEG_INF)
            elif c["mode"] == "in_sequence":
                s = jnp.where(c["kpos"] >= 0, s, NEG_INF)
            c["s"] = s

        def softmax_update(c):
            m = m_s[c["slot"]]
            m_new = jnp.maximum(m, jnp.max(c["s"], axis=0, keepdims=True))
            c["p"] = jnp.exp2(c.pop("s") - m_new).astype(BF16)
            c["alpha"] = jnp.exp2(m - m_new)
            m_s[c["slot"]] = m_new

        def accumulate(c):
            acc_s[c["slot"]] = c["alpha"] * acc_s[c["slot"]] + _dot(c["v"], c.pop("p"))

        for c in chunks[:ahead]:
            scores(c)
        for i, c in enumerate(chunks):
            softmax_update(c)
            if i + ahead < len(chunks):
                scores(chunks[i + ahead])
            accumulate(c)

    def result(branch, h2, g):
        slot = (h2 * 2 + branch) * grp + g
        return acc_s[slot, 0:dh, :] / acc_s[slot, dh:dh + 1, :]

    m_s[...] = jnp.full(m_s.shape, NEG_INF, F32)
    acc_s[...] = jnp.zeros(acc_s.shape, F32)
    q_augs, o_cmps = [], []
    cmp_scores = [[_dot(kc_ref[0, h2, :, 0:dh], q_ref[0, (h2 * grp + g) * dh:(h2 * grp + g + 1) * dh, :])
                   for g in range(grp)] for h2 in range(2)]

    for h2 in range(2):
        head_rows = lambda ref, g: ref[0, (h2 * grp + g) * dh:(h2 * grp + g + 1) * dh, :]

        p_grp = jnp.zeros((n_cmp_pad, tq), F32)
        o_cmp = []
        for g in range(grp):
            s = jnp.where(cmp_mask, cmp_scores[h2][g], NEG_INF)
            e = jnp.where(cmp_mask, jnp.exp2(s - jnp.max(s, axis=0, keepdims=True)), 0.0)
            inv = 1.0 / jnp.maximum(jnp.sum(e, axis=0, keepdims=True), 1e-30)
            o_cmp.append(_dot(vcmp_s[h2], e.astype(BF16)) * inv)
            p_grp = p_grp + e * inv
        i3 = _dot(wt_ref[...], jnp.concatenate(_split3(p_grp), axis=1))
        imp = i3[:, 0:tq] + i3[:, tq:2 * tq] + i3[:, 2 * tq:]
        imp = jnp.where(blk > cur, -1.0, jnp.where(forced, FORCE_SCORE, imp))
        rows = 8
        ranks = []
        for r0 in range(0, n_sel, rows):
            mine = imp[r0:r0 + rows]
            row_id = blk[r0:r0 + rows]
            rank = jnp.zeros((rows, tq), F32)
            for jp in range(n_sel):
                other = imp[jp:jp + 1, :]
                ge = lambda: jnp.where(other >= mine, 1.0, 0.0)
                gt_ = lambda: jnp.where(other > mine, 1.0, 0.0)
                if jp < r0:
                    rank = rank + ge()
                elif jp >= r0 + rows:
                    rank = rank + gt_()
                else:
                    rank = rank + jnp.where(row_id > jp, ge(), gt_())
            ranks.append(rank)
        rank = jnp.concatenate(ranks, axis=0)
        sel_bias = jnp.where(rank < float(SEL_TOPK), 0.0, NEG_INF).astype(BF16)
        pad = jnp.zeros((dh - n_sel, tq), BF16)
        q_augs.append([jnp.concatenate([head_rows(qr_ref, g), sel_bias, pad], axis=0) for g in range(grp)])
        o_cmps.append(o_cmp)

    sel, win = 0, 1

    no_bias = jnp.zeros((dh, tq), BF16)
    q_wins = [[jnp.concatenate([qr_ref[0, (h2 * grp + g) * dh:(h2 * grp + g + 1) * dh, :], no_bias], axis=0)
               for g in range(grp)] for h2 in range(2)]
    run_tiles([(win, kwin_s, vwin_s, q_wins, h2, qi - 2, "window_start") for h2 in range(2)]
              + [(win, kwin_s, vwin_s, q_wins, h2, qi - 1, "in_sequence") for h2 in range(2)]
              + [(win, kwin_s, vwin_s, q_wins, h2, qi, "causal") for h2 in range(2)]
              + [(sel, ksel_s, vsel_s, q_augs, h2, qi, "causal") for h2 in range(2)],
              ahead=NSA_SCORES_AHEAD - 1)

    def sel_tiles(kts):
        return [(sel, ksel_s, vsel_s, q_augs, h2, kt, None) for kt in kts for h2 in range(2)]

    group = NSA_SEL_TILES_PER_TRIP

    def sel_body(trip, carry):
        run_tiles(sel_tiles([trip * group + j for j in range(group)]))
        return carry

    lax.fori_loop(0, qi // group, sel_body, 0)
    done = (qi // group) * group
    size = group // 2
    while size >= 1:
        @pl.when(((qi - done) & size) != 0)
        def _(done=done, size=size):
            run_tiles(sel_tiles([done + j for j in range(size)]))

        done = done + ((qi - done) & size)
        size //= 2

    for h2 in range(2):
        for g in range(grp):
            hg = h2 * grp + g
            gate = lambda branch: gt_ref[0, branch * heads + hg:branch * heads + hg + 1, :]
            o_ref[0, hg * dh:(hg + 1) * dh, :] = (
                gate(0) * o_cmps[h2][g] + gate(1) * result(sel, h2, g) + gate(2) * result(win, h2, g)
            ).astype(o_ref.dtype)


def _nsa_attention(q_t, qr_t, gates_t, kvc, kv):
    b, d, s = q_t.shape
    tq, tk = NSA_TQ, NSA_TK
    assert WINDOW == 2 * tq and tq % SEL_BLOCK == 0 and tq % tk == 0
    dh = NSA_DH
    pairs = NSA_KV_HEADS // 2
    heads = 2 * NSA_GROUP
    pw = heads * dh
    n_cmp_pad = kvc.shape[2]
    n_sel = s // SEL_BLOCK
    wt = jnp.asarray(_cmp_to_sel_weights_t(n_cmp_pad, n_sel), BF16)
    onehot = np.zeros((s, dh), np.float32)
    onehot[np.arange(s), np.arange(s) // SEL_BLOCK] = 1.0
    kv_spec = lambda grp: pl.BlockSpec((1, s, LANE), lambda bi, p, i: (bi, 0, pairs * grp + p))
    return pl.pallas_call(
        functools.partial(_nsa_kernel, tq=tq, tk=tk),
        out_shape=jax.ShapeDtypeStruct((b, d, s), BF16),
        grid=(b, pairs, s // tq),
        in_specs=[
            pl.BlockSpec((1, pw, tq), lambda bi, p, i: (bi, p, i)),
            pl.BlockSpec((1, pw, tq), lambda bi, p, i: (bi, p, i)),
            pl.BlockSpec((1, LANE, tq), lambda bi, p, i: (bi, p, i)),
            pl.BlockSpec((1, 2, n_cmp_pad, LANE), lambda bi, p, i: (bi, p, 0, 0)),
            kv_spec(0), kv_spec(1), kv_spec(2), kv_spec(3),
            _const_spec(wt.shape),
            _const_spec(onehot.shape),
        ],
        out_specs=pl.BlockSpec((1, pw, tq), lambda bi, p, i: (bi, p, i)),
        scratch_shapes=[
            pltpu.VMEM((2, s, 2 * dh), BF16),
            pltpu.VMEM((2, s, 2 * dh), BF16),
            pltpu.VMEM((2, s // tk, NSA_ACC_ROWS, tk), BF16),
            pltpu.VMEM((2, s // tk, NSA_ACC_ROWS, tk), BF16),
            pltpu.VMEM((2, dh, n_cmp_pad), BF16),
            pltpu.VMEM((2 * heads, 1, tq), F32),
            pltpu.VMEM((2 * heads, NSA_ACC_ROWS, tq), F32),
        ],
        compiler_params=_params("parallel", "parallel", "arbitrary"),
    )(q_t, qr_t, gates_t, kvc, kv, kv, kv, kv, wt, jnp.asarray(onehot, BF16))


def _rope_tables(seq, width):
    half = NSA_DH // 2
    inv = ROPE_THETA ** (-jnp.arange(half, dtype=F32) / half)
    ang = jnp.arange(seq, dtype=F32)[:, None] * inv[None, :]
    cos, sin = jnp.cos(ang), jnp.sin(ang)
    reps = width // NSA_DH
    cos_full = jnp.tile(jnp.concatenate([cos, cos], axis=1), (1, reps))
    sin_signed = jnp.tile(jnp.concatenate([-sin, sin], axis=1), (1, reps))
    return cos_full, sin_signed


def _nsa_gate_weights(w_gate):
    d = w_gate.shape[0]
    pairs = NSA_KV_HEADS // 2
    heads = 2 * NSA_GROUP
    wg = w_gate.reshape(d, 3, pairs, heads).transpose(0, 2, 1, 3).reshape(d, pairs, 3 * heads)
    return jnp.pad(wg, ((0, 0), (0, 0), (0, LANE - 3 * heads))).reshape(d, pairs * LANE)


def kernel(x, norm_mix, norm_ffn, gla_w_in, gla_w_alpha_up, gla_b_alpha, gla_norm, gla_w_o, kv_norm, nsa_w_kv, cmp_pe_k, cmp_pe_v, cmp_k_w1, cmp_k_w2, cmp_v_w1, cmp_v_w2, nsa_w_in, nsa_w_o, ffn_w_up, ffn_conv_w, ffn_conv_b, ffn_w_down, norm_final):
    b, s, d = x.shape
    n = b * s
    x2 = x.reshape(n, d)

    n_main = 2 * GLA_HEADS * GLA_DK + 2 * GLA_HEADS * GLA_DV
    w_in = gla_w_in[0]
    n_pairs = len(GATE_PIECE_PAIRS)
    lane_pad = LANE - n_pairs * GLA_GATE_RANK
    w_low = jnp.pad(jnp.tile(w_in[:, n_main:], (1, n_pairs)), ((0, 0), (0, lane_pad))).astype(BF16)
    w_up_pieces = _split3(gla_w_alpha_up[0])
    w_up6 = jnp.pad(jnp.concatenate([w_up_pieces[iw] for _, iw in GATE_PIECE_PAIRS], axis=0),
                    ((0, lane_pad), (0, 0)))
    proj, log_a = _gla_front(x2, norm_mix[0], w_in[:, :n_main].astype(BF16), w_low, w_up6, gla_b_alpha[0])
    o = _gla_core(proj.reshape(b, s, n_main), log_a.reshape(b, s, -1), gla_norm[0])
    x2 = _conv_ffn(x, o, gla_w_o[0].astype(BF16), norm_ffn[0], ffn_w_up[0].astype(BF16), ffn_conv_w[0],
                   ffn_conv_b[0], ffn_w_down[0].astype(BF16)).reshape(n, d)

    hk, dh = NSA_KV_HEADS, NSA_DH
    nq = NSA_HEADS * dh
    w_nsa = nsa_w_in[0]
    kv, k16, v16, q_t, qr_t, gates_t = _nsa_front(
        x2, kv_norm, norm_mix[1], nsa_w_kv.astype(BF16), w_nsa[:, :nq].astype(BF16),
        _nsa_gate_weights(w_nsa[:, nq:]).astype(BF16), _rope_tables(s, hk * dh), seq=s,
        q_scale=dh ** -0.5 * LOG2E)
    kvc = _compress(k16, v16, cmp_pe_k, cmp_pe_v, cmp_k_w1.astype(BF16), cmp_v_w1.astype(BF16),
                    cmp_k_w2.astype(BF16), cmp_v_w2.astype(BF16), rows_per_seq=s // CMP_STRIDE)

    o_t = _nsa_attention(q_t, qr_t, gates_t, kvc, kv.reshape(b, s, -1))
    return _conv_ffn(x2.reshape(b, s, d), o_t, nsa_w_o[0].astype(BF16), norm_ffn[1], ffn_w_up[1].astype(BF16),
                     ffn_conv_w[1], ffn_conv_b[1], ffn_w_down[1].astype(BF16), norm_final, mix_transposed=True)
```

```python
import functools
import math

import numpy as np
import jax
import jax.numpy as jnp
from jax import lax
from jax.experimental import pallas as pl
from jax.experimental.pallas import tpu as pltpu

F32 = jnp.float32
BF16 = jnp.bfloat16

EPS = 1e-6
NEG_INF = -1e30
ROPE_THETA = 10000.0
LOG2E = math.log2(math.e)

LANE = 128
BF16_SUBLANES = 16
VMEM_LIMIT_BYTES = 56 * 1024 * 1024

GLA_HEADS = 4
GLA_DK = 128
GLA_DV = 256
GLA_GATE_RANK = 16
GLA_TAU = 16.0
GLA_CHUNK = 64
GLA_LEVEL_GROUP = 4
GATE_PIECE_PAIRS = ((0, 0), (0, 1), (1, 0), (0, 2), (1, 1), (2, 0))
NSA_HEADS = 16
NSA_KV_HEADS = 4
NSA_GROUP = 4
NSA_DH = 64
CMP_BLOCK = 32
CMP_STRIDE = 16
SEL_BLOCK = 64
SEL_TOPK = 16
WINDOW = 512
FORCE_SCORE = 1e4
CONV_WIDTH = 3
CONV_HALO = 16

PROJ_ROWS = 1024
PROJ_COLS = 512
GLA_ROWS = 1024
GLA_CHUNKS_PER_TRIP = 8
FFN_ROWS = 1024
FFN_COLS = 256
FFN_TILES_PER_TRIP = 4
NSA_TQ = 256
NSA_TK = 256
NSA_SEL_TILES_PER_TRIP = 4
NSA_SCORES_AHEAD = 6
NSA_ACC_ROWS = NSA_DH + BF16_SUBLANES


def _dot(a, b):
    return jnp.dot(a, b, preferred_element_type=F32)


def _dot_nt(a, b):
    return lax.dot_general(a, b, (((1,), (1,)), ((), ())), preferred_element_type=F32)


def _dot_tn(a, b):
    return lax.dot_general(a, b, (((0,), (0,)), ((), ())), preferred_element_type=F32)


def _rms(x, g):
    return x * lax.rsqrt(jnp.mean(x * x, axis=-1, keepdims=True) + EPS) * g


def _split3(x):
    hi = x.astype(BF16)
    r1 = x - hi.astype(F32)
    mid = r1.astype(BF16)
    lo = (r1 - mid.astype(F32)).astype(BF16)
    return hi, mid, lo


def _rope_tile(x, cos, sin_signed):
    w = x.shape[-1]
    lane = lax.broadcasted_iota(jnp.int32, x.shape, 1)
    first_half = (lane & (NSA_DH - 1)) < (NSA_DH // 2)
    partner = jnp.where(first_half, pltpu.roll(x, w - NSA_DH // 2, 1), pltpu.roll(x, NSA_DH // 2, 1))
    return x * cos + partner * sin_signed


def _params(*sem):
    return pltpu.CompilerParams(dimension_semantics=sem, vmem_limit_bytes=VMEM_LIMIT_BYTES)


def _const_spec(shape):
    nd = len(shape)
    return pl.BlockSpec(shape, lambda *_: (0,) * nd, pipeline_mode=pl.Buffered(1))


def _gla_front_kernel(x_ref, g_ref, w_ref, wl_ref, wu_ref, b_ref, o_ref, la_ref):
    hn = _rms(x_ref[...], g_ref[...]).astype(BF16)
    a_rep = _dot(hn, wl_ref[...])
    hi, mid, lo = _split3(a_rep)
    group = lax.broadcasted_iota(jnp.int32, a_rep.shape, 1) // GLA_GATE_RANK
    a_piece = [g for g, (ia, _) in enumerate(GATE_PIECE_PAIRS)]
    is_piece = lambda p: functools.reduce(
        jnp.logical_or, [group == g for g in a_piece if GATE_PIECE_PAIRS[g][0] == p])
    a6 = jnp.where(is_piece(2), lo, jnp.where(is_piece(1), mid, hi))
    z = b_ref[...] + _dot(a6, wu_ref[...])
    log_sig = jnp.minimum(z, 0.0) - jnp.log(1.0 + jnp.exp(-jnp.abs(z)))
    la_ref[...] = log_sig * (1.0 / GLA_TAU)
    m = w_ref.shape[1]
    for c in range(0, m, PROJ_COLS):
        o_ref[:, c:c + PROJ_COLS] = _dot(hn, w_ref[:, c:c + PROJ_COLS]).astype(o_ref.dtype)


def _gla_front(x2, g, w_main, w_low, w_up6, b):
    n, d = x2.shape
    m = w_main.shape[1]
    hk = w_up6.shape[-1]
    tm = PROJ_ROWS
    return pl.pallas_call(
        _gla_front_kernel,
        out_shape=[jax.ShapeDtypeStruct((n, m), BF16), jax.ShapeDtypeStruct((n, hk), F32)],
        grid=(n // tm,),
        in_specs=[
            pl.BlockSpec((tm, d), lambda i: (i, 0)),
            _const_spec((1, d)),
            _const_spec((d, m)),
            _const_spec((d, LANE)),
            _const_spec((LANE, hk)),
            _const_spec((1, hk)),
        ],
        out_specs=[pl.BlockSpec((tm, m), lambda i: (i, 0)), pl.BlockSpec((tm, hk), lambda i: (i, 0))],
        compiler_params=_params("parallel"),
    )(x2, g.reshape(1, d), w_main, w_low, w_up6, b.reshape(1, hk))


def _gla_constants():
    c = GLA_CHUNK
    idx = np.arange(c)
    row, col = idx[:, None], idx[None, :]
    mats = [col <= row, col > row]
    masks = []
    h = 1
    while h < c:
        blk = idx // (2 * h)
        upper = (idx % (2 * h)) >= h
        r = (blk * 2 * h + h - 1)[:, None]
        mats.append(np.where(upper[:, None], (col > r) & (col <= row), (col > row) & (col <= r)))
        masks.append((blk[:, None] == blk[None, :]) & upper[:, None] & (~upper[None, :]))
        h *= 2
    masks.append(np.eye(c, dtype=bool))
    strips = np.zeros((len(masks), c, LANE), np.float32)
    for lvl, m in enumerate(masks):
        half = (lvl % GLA_LEVEL_GROUP) % (LANE // c)
        strips[lvl, :, half * c:(half + 1) * c] = m
    return np.concatenate(mats, 0).astype(np.float32), strips


def _gla_kernel(q_ref, k_ref, v_ref, r_ref, g_ref, gs_ref, mk_ref, gn_ref, o_ref, st_ref, *, n_chunks):
    c, dk, dv = GLA_CHUNK, GLA_DK, GLA_DV
    n_levels = mk_ref.shape[0]
    per_strip = LANE // c
    q_scale = dk ** -0.5

    @pl.when(pl.program_id(1) == 0)
    def _():
        st_ref[...] = jnp.zeros_like(st_ref)

    def decays(h, r0, t):
        ks = slice(h * dk, (h + 1) * dk)
        t["q"] = q_ref[0, pl.ds(r0, c), ks].astype(F32) * q_scale
        t["k"] = k_ref[0, pl.ds(r0, c), ks].astype(F32)
        t["v"] = v_ref[0, pl.ds(r0, c), h * dv:(h + 1) * dv]
        hi, mid, _ = _split3(g_ref[0, pl.ds(r0, c), ks])
        t["e2"] = _dot(gs_ref[...], jnp.concatenate([hi, mid], axis=1))

    def level_products(t):
        e2 = t.pop("e2")
        x = t["x"] = jnp.exp(e2[:, :dk] + e2[:, dk:])
        q, k = t["q"], t["k"]
        t["prods"] = []
        for first in range(0, n_levels, GLA_LEVEL_GROUP):
            lvls = list(range(first, min(first + GLA_LEVEL_GROUP, n_levels)))
            factor = lambda lvl: x[(2 + lvl) * c:(3 + lvl) * c] if lvl < n_levels - 1 else 1.0
            q_stack = jnp.concatenate([(q * factor(lvl)).astype(BF16) for lvl in lvls], axis=0)
            k_stack = jnp.concatenate([(k * factor(lvl)).astype(BF16) for lvl in lvls]
                                      + [jnp.zeros((c, dk), BF16)] * (GLA_LEVEL_GROUP - len(lvls)), axis=0)
            t["prods"].append((lvls, _dot_nt(q_stack, k_stack)))
        t["q_inter"] = (q * x[0:c]).astype(BF16)
        t["k_state"] = (k * x[c:2 * c]).astype(BF16)

    def outputs(h, t):
        strip = jnp.zeros((c, LANE), F32)
        for lvls, prod in t.pop("prods"):
            for j, lvl in enumerate(lvls):
                col = (j // per_strip) * LANE
                strip = strip + prod[j * c:(j + 1) * c, col:col + LANE] * mk_ref[lvl]
        att = strip[:, 0:c]
        for part in range(1, per_strip):
            att = att + strip[:, part * c:(part + 1) * c]
        st = st_ref[h]
        t["o"] = _dot(att.astype(BF16), t["v"]) + _dot_nt(t["q_inter"], st.astype(BF16))
        st_ref[h] = st * t["x"][c - 1:c] + _dot_tn(t["v"], t["k_state"])

    def finish(h, r0, t):
        o = t["o"]
        o = o * lax.rsqrt(jnp.mean(o * o, axis=-1, keepdims=True) + EPS) * gn_ref[h:h + 1, :]
        r = r_ref[0, pl.ds(r0, c), h * dv:(h + 1) * dv].astype(F32)
        o_ref[0, pl.ds(r0, c), h * dv:(h + 1) * dv] = (o * (r * jax.nn.sigmoid(r))).astype(o_ref.dtype)

    def body(ci, carry):
        items = [(h, pl.multiple_of((ci * GLA_CHUNKS_PER_TRIP + sub) * c, c), dict())
                 for sub in range(GLA_CHUNKS_PER_TRIP) for h in range(GLA_HEADS)]
        for h, r0, t in items:
            decays(h, r0, t)
        for h, r0, t in items:
            level_products(t)
        for h, r0, t in items:
            outputs(h, t)
        for h, r0, t in items:
            finish(h, r0, t)
        return carry

    lax.fori_loop(0, n_chunks // GLA_CHUNKS_PER_TRIP, body, 0)


def _gla_core(proj, log_a, gla_norm):
    b, s, _ = proj.shape
    h, dk, dv = GLA_HEADS, GLA_DK, GLA_DV
    ts = GLA_ROWS
    gs, mk = _gla_constants()
    return pl.pallas_call(
        functools.partial(_gla_kernel, n_chunks=ts // GLA_CHUNK),
        out_shape=jax.ShapeDtypeStruct((b, s, h * dv), BF16),
        grid=(b, s // ts),
        in_specs=[
            pl.BlockSpec((1, ts, h * dk), lambda i, j: (i, j, 0)),
            pl.BlockSpec((1, ts, h * dk), lambda i, j: (i, j, 1)),
            pl.BlockSpec((1, ts, h * dv), lambda i, j: (i, j, 1)),
            pl.BlockSpec((1, ts, h * dv), lambda i, j: (i, j, 2)),
            pl.BlockSpec((1, ts, h * dk), lambda i, j: (i, j, 0)),
            _const_spec(gs.shape),
            _const_spec(mk.shape),
            _const_spec((h, dv)),
        ],
        out_specs=pl.BlockSpec((1, ts, h * dv), lambda i, j: (i, j, 0)),
        scratch_shapes=[pltpu.VMEM((h, dv, dk), F32)],
        compiler_params=_params("parallel", "arbitrary"),
    )(proj, proj, proj, proj, log_a, jnp.asarray(gs, BF16), jnp.asarray(mk), gla_norm)


def _ffn_kernel(*refs, tm, final, a_transposed):
    if final:
        (rm_ref, rh_ref, am_ref, ah_ref, wo_ref, gn_ref, wup_ref, cw_ref, cb_ref, wd_ref, gf_ref,
         o_ref, hn_ref, u_ref) = refs
    else:
        (rm_ref, rh_ref, am_ref, ah_ref, wo_ref, gn_ref, wup_ref, cw_ref, cb_ref, wd_ref,
         o_ref, hn_ref, u_ref) = refs
    halo = CONV_HALO
    nf = wd_ref.shape[0]
    g = gn_ref[...]
    wo = wo_ref[...]
    half = tm // 2
    xs = []
    for j in range(2):
        rows = slice(j * half, (j + 1) * half)
        mix = _dot_tn(am_ref[0, :, rows], wo) if a_transposed else _dot(am_ref[0, rows, :], wo)
        xs.append(rm_ref[0, rows, :] + mix)
    if a_transposed:
        x_halo = rh_ref[0] + _dot_tn(ah_ref[0], wo)[ah_ref.shape[2] - halo:, :]
    else:
        x_halo = rh_ref[0] + _dot(ah_ref[0], wo)
    hn_ref[0:halo, :] = _rms(x_halo, g).astype(BF16)
    for j in range(2):
        hn_ref[halo + j * half:halo + (j + 1) * half, :] = _rms(xs[j], g).astype(BF16)
        o_ref[0, j * half:(j + 1) * half, :] = xs[j]
    keep = jnp.where(pl.program_id(1) > 0, 1.0, 0.0)

    def up_project(fi, slot):
        hn = hn_ref[...]
        for gv in range(2):
            tf = u_ref.shape[-1]
            col = pl.multiple_of((gv * nf + fi) * tf, tf)
            u = _dot(hn, wup_ref[:, pl.ds(col, tf)])
            u_ref[slot, gv] = u
            u_ref[slot, gv, 0:halo, :] = u[0:halo] * keep

    def activation(fi, slot):
        outs = []
        for gv in range(2):
            out = cb_ref[gv, fi]
            for tap in range(CONV_WIDTH):
                start = halo - (CONV_WIDTH - 1) + tap
                out = out + u_ref[slot, gv, start:start + tm, :] * cw_ref[gv, fi, tap:tap + 1, :]
            outs.append(out)
        gate, val = outs
        return (gate * jax.nn.sigmoid(gate) * val).astype(BF16)

    def run(tiles):
        up_project(tiles[0], 0)
        for j, fi in enumerate(tiles):
            if j + 1 < len(tiles):
                up_project(tiles[j + 1], (j + 1) % 2)
            o_ref[0] += _dot(activation(fi, j % 2), wd_ref[fi])

    group = FFN_TILES_PER_TRIP

    def trip(ti, carry):
        run([ti * group + j for j in range(group)])
        return carry

    lax.fori_loop(0, nf // group, trip, 0)
    if nf % group:
        run(list(range(nf - nf % group, nf)))
    if final:
        o_ref[0] = _rms(o_ref[0], gf_ref[...])


def _conv_ffn(res3, mix, w_o, g, w_up, conv_w, conv_b, w_down, final_gain=None, *, mix_transposed=False):
    b, s, d = res3.shape
    k = w_o.shape[0]
    ffn = w_down.shape[0]
    tm, tf = FFN_ROWS, FFN_COLS
    nf = ffn // tf
    halo = CONV_HALO
    final = final_gain is not None
    w_up_t = w_up
    conv_w_t = conv_w.reshape(CONV_WIDTH, 2, nf, tf).transpose(1, 2, 0, 3)
    conv_b_t = conv_b.reshape(2, nf, 1, tf)
    w_down_t = w_down.reshape(nf, tf, d)
    halo_rows = lambda bi, i: (bi, jnp.maximum(i * (tm // halo) - 1, 0), 0)
    if mix_transposed:
        mix_specs = [pl.BlockSpec((1, k, tm), lambda bi, i: (bi, 0, i)),
                     pl.BlockSpec((1, k, LANE), lambda bi, i: (bi, 0, jnp.maximum(i * (tm // LANE) - 1, 0)))]
    else:
        mix_specs = [pl.BlockSpec((1, tm, k), lambda bi, i: (bi, i, 0)), pl.BlockSpec((1, halo, k), halo_rows)]
    in_specs = [
        pl.BlockSpec((1, tm, d), lambda bi, i: (bi, i, 0)),
        pl.BlockSpec((1, halo, d), halo_rows),
        *mix_specs,
        _const_spec((k, d)),
        _const_spec((1, d)),
        _const_spec(w_up_t.shape),
        _const_spec(conv_w_t.shape),
        _const_spec(conv_b_t.shape),
        _const_spec(w_down_t.shape),
    ]
    args = [res3, res3, mix, mix, w_o, g.reshape(1, d), w_up_t, conv_w_t, conv_b_t, w_down_t]
    if final:
        in_specs.append(_const_spec((1, d)))
        args.append(final_gain.reshape(1, d))
    return pl.pallas_call(
        functools.partial(_ffn_kernel, tm=tm, final=final, a_transposed=mix_transposed),
        out_shape=jax.ShapeDtypeStruct((b, s, d), F32),
        grid=(b, s // tm),
        in_specs=in_specs,
        out_specs=pl.BlockSpec((1, tm, d), lambda bi, i: (bi, i, 0)),
        scratch_shapes=[
            pltpu.VMEM((tm + halo, d), BF16),
            pltpu.VMEM((2, 2, tm + halo, tf), F32),
        ],
        compiler_params=_params("parallel", "parallel"),
    )(*args)


def _nsa_front_kernel(x_ref, gkv_ref, gq_ref, wkv_ref, wq_ref, wg_ref, cos_ref, sin_ref,
                      kv_ref, k16_ref, v16_ref, q_ref, qr_ref, gt_ref, raw_s, *, q_scale):
    x = x_ref[...]
    y = x * lax.rsqrt(jnp.mean(x * x, axis=-1, keepdims=True) + EPS)
    hn_kv = (y * gkv_ref[...]).astype(BF16)
    hn_q = (y * gq_ref[...]).astype(BF16)
    cos, sin = cos_ref[...], sin_ref[...]
    rope_w = cos.shape[1]

    gt_ref[0] = jax.nn.sigmoid(_dot(hn_q, wg_ref[...])).T
    for grp in range(wq_ref.shape[1] // rope_w):
        acc = _dot(hn_q, wq_ref[:, grp * rope_w:(grp + 1) * rope_w]) * q_scale
        q_ref[0, grp * rope_w:(grp + 1) * rope_w, :] = acc.T.astype(q_ref.dtype)
        qr_ref[0, grp * rope_w:(grp + 1) * rope_w, :] = _rope_tile(acc, cos, sin).T.astype(qr_ref.dtype)
    n_raw, strips, groups = raw_s.shape[0], raw_s.shape[1], raw_s.shape[2] // CMP_STRIDE
    for grp, out_ref in ((0, k16_ref), (1, v16_ref)):
        acc = _dot(hn_kv, wkv_ref[:, grp * rope_w:(grp + 1) * rope_w])
        for j in range(strips):
            raw_s[grp, j] = acc[:, j * LANE:(j + 1) * LANE]
        for r in range(CMP_STRIDE):
            for j in range(strips):
                out_ref[:, r * rope_w + j * LANE:r * rope_w + (j + 1) * LANE] = (
                    raw_s[grp, j, pl.ds(r, groups, stride=CMP_STRIDE), :].astype(out_ref.dtype))
    for grp in (2, 4, 3, 5):
        acc = _dot(hn_kv, wkv_ref[:, grp * rope_w:(grp + 1) * rope_w])
        if grp in (2, 4):
            acc = _rope_tile(acc, cos, sin)
        kv_ref[:, (grp - n_raw) * rope_w:(grp - n_raw + 1) * rope_w] = acc.astype(kv_ref.dtype)


def _nsa_front(x2, g_kv, g_q, w_kv, w_q, w_g, rope, *, seq, q_scale):
    n, d = x2.shape
    tm = PROJ_ROWS
    per_seq = seq // tm
    batch = n // seq
    rope_w = rope[0].shape[1]
    row = lambda m: pl.BlockSpec((tm, m), lambda i: (i, 0))
    col = lambda m: pl.BlockSpec((1, m, tm), lambda i: (i // per_seq, 0, i % per_seq))
    n_att = w_kv.shape[1] - 2 * rope_w
    grouped = CMP_STRIDE * rope_w
    return pl.pallas_call(
        functools.partial(_nsa_front_kernel, q_scale=q_scale),
        out_shape=[
            jax.ShapeDtypeStruct((n, n_att), BF16),
            jax.ShapeDtypeStruct((n // CMP_STRIDE, grouped), BF16),
            jax.ShapeDtypeStruct((n // CMP_STRIDE, grouped), BF16),
            jax.ShapeDtypeStruct((batch, w_q.shape[1], seq), BF16),
            jax.ShapeDtypeStruct((batch, w_q.shape[1], seq), BF16),
            jax.ShapeDtypeStruct((batch, w_g.shape[1], seq), F32),
        ],
        grid=(n // tm,),
        in_specs=[
            row(d), _const_spec((1, d)), _const_spec((1, d)),
            _const_spec(w_kv.shape), _const_spec(w_q.shape), _const_spec(w_g.shape),
            pl.BlockSpec((tm, rope_w), lambda i: (i % per_seq, 0)),
            pl.BlockSpec((tm, rope_w), lambda i: (i % per_seq, 0)),
        ],
        out_specs=[row(n_att),
                   pl.BlockSpec((tm // CMP_STRIDE, grouped), lambda i: (i, 0)),
                   pl.BlockSpec((tm // CMP_STRIDE, grouped), lambda i: (i, 0)),
                   col(w_q.shape[1]), col(w_q.shape[1]), col(w_g.shape[1])],
        scratch_shapes=[pltpu.VMEM((2, rope_w // LANE, tm, LANE), F32)],
        compiler_params=_params("parallel"),
    )(x2, g_kv.reshape(1, d), g_q.reshape(1, d), w_kv, w_q, w_g, *rope)


def _compress_kernel(tk_ref, tv_ref, w1k_ref, w1v_ref, bk_ref, bv_ref, w2k_ref, w2v_ref, o_ref):
    rows = tk_ref.shape[0]
    dh = NSA_DH
    width = NSA_KV_HEADS * dh
    row = lax.broadcasted_iota(jnp.int32, (rows, 1), 0)
    valid = jnp.where(row < rows - 1, 1.0, 0.0)

    def first_layer(t_ref, w1_ref, h):
        y = None
        for r in range(CMP_STRIDE):
            part = _dot(t_ref[:, r * width + h * dh:r * width + (h + 1) * dh], w1_ref[r * dh:(r + 1) * dh, :])
            y = part if y is None else y + part
        return y

    def second_layer(y, b_ref, w2_ref):
        hid = y.shape[1] // 2
        pre = y[:, :hid] + pltpu.roll(y[:, hid:], rows - 1, 0) + b_ref[0:1, :]
        return _dot(jax.nn.gelu(pre).astype(BF16), w2_ref[...]) * valid

    for h in range(NSA_KV_HEADS):
        yk = first_layer(tk_ref, w1k_ref, h)
        yv = first_layer(tv_ref, w1v_ref, h)
        o_ref[0, h] = jnp.concatenate(
            [second_layer(yk, bk_ref, w2k_ref), second_layer(yv, bv_ref, w2v_ref)], axis=1).astype(o_ref.dtype)


def _pe_bias_kernel(pk_ref, pv_ref, w1k_ref, w1v_ref, ok_ref, ov_ref):
    ok_ref[...] = _dot(pk_ref[...], w1k_ref[...])
    ov_ref[...] = _dot(pv_ref[...], w1v_ref[...])


def _compress(tk, tv, pe_k, pe_v, w1k, w1v, w2k, w2v, *, rows_per_seq):
    r, width = tk.shape
    hid = w1k.shape[1]
    dh = w2k.shape[1]
    kdim = CMP_STRIDE * dh
    batch = r // rows_per_seq
    sub = 8
    pk = jnp.broadcast_to(pe_k.reshape(1, -1), (sub, 2 * kdim)).astype(BF16)
    pv = jnp.broadcast_to(pe_v.reshape(1, -1), (sub, 2 * kdim)).astype(BF16)
    bk, bv = pl.pallas_call(
        _pe_bias_kernel,
        out_shape=[jax.ShapeDtypeStruct((sub, hid), F32)] * 2,
    )(pk, pv, w1k, w1v)
    w1k_cat = jnp.concatenate([w1k[:kdim], w1k[kdim:]], axis=1)
    w1v_cat = jnp.concatenate([w1v[:kdim], w1v[kdim:]], axis=1)
    return pl.pallas_call(
        _compress_kernel,
        out_shape=jax.ShapeDtypeStruct((batch, NSA_KV_HEADS, rows_per_seq, 2 * dh), BF16),
        grid=(batch,),
        in_specs=[
            pl.BlockSpec((rows_per_seq, width), lambda i: (i, 0)),
            pl.BlockSpec((rows_per_seq, width), lambda i: (i, 0)),
            _const_spec((kdim, 2 * hid)),
            _const_spec((kdim, 2 * hid)),
            _const_spec((sub, hid)),
            _const_spec((sub, hid)),
            _const_spec((hid, dh)),
            _const_spec((hid, dh)),
        ],
        out_specs=pl.BlockSpec((1, NSA_KV_HEADS, rows_per_seq, 2 * dh), lambda i: (i, 0, 0, 0)),
        compiler_params=_params("parallel"),
    )(tk, tv, w1k_cat, w1v_cat, bk, bv, w2k, w2v)


def _cmp_to_sel_weights_t(n_cmp_pad, n_sel):
    c0 = np.arange(n_cmp_pad)[None, :] * CMP_STRIDE
    s0 = np.arange(n_sel)[:, None] * SEL_BLOCK
    ov = np.clip(np.minimum(c0 + CMP_BLOCK, s0 + SEL_BLOCK) - np.maximum(c0, s0), 0, None)
    return (ov / CMP_BLOCK).astype(np.float32)


def _nsa_kernel(q_ref, qr_ref, gt_ref, kc_ref, ks_ref, vs_ref, kw_ref, vw_ref, wt_ref, oh_ref, o_ref,
                ksel_s, kwin_s, vsel_s, vwin_s, vcmp_s, m_s, acc_s, *, tq, tk):
    grp, dh = NSA_GROUP, NSA_DH
    heads = 2 * grp
    qi = pl.program_id(2)
    t0 = qi * tq
    n_sel = wt_ref.shape[0]
    n_cmp_pad = wt_ref.shape[1]
    seq = ks_ref.shape[1]

    @pl.when(qi == 0)
    def _():
        ones_row = jnp.where(lax.broadcasted_iota(jnp.int32, (BF16_SUBLANES, tk), 0) == 0, 1.0, 0.0).astype(BF16)
        for h2 in range(2):
            cols = slice(h2 * dh, (h2 + 1) * dh)
            ksel_s[h2, :, 0:dh] = ks_ref[0, :, cols]
            ksel_s[h2, :, dh:2 * dh] = oh_ref[...]
            kwin_s[h2, :, 0:dh] = kw_ref[0, :, cols]
            kwin_s[h2, :, dh:2 * dh] = jnp.zeros((seq, dh), BF16)
            vcmp_s[h2] = kc_ref[0, h2, :, dh:2 * dh].astype(F32).T.astype(BF16)

        def build(j, carry):
            r0 = pl.multiple_of(j * tk, tk)
            for src, dst in ((vs_ref, vsel_s), (vw_ref, vwin_s)):
                blk_t = src[0, pl.ds(r0, tk), :].astype(F32).T
                for h2 in range(2):
                    dst[h2, j, 0:dh, :] = blk_t[h2 * dh:(h2 + 1) * dh].astype(BF16)
                    dst[h2, j, dh:, :] = ones_row
            return carry

        lax.fori_loop(0, seq // tk, build, 0)

    tpos = t0 + lax.broadcasted_iota(jnp.int32, (1, tq), 1)
    key_iota = lax.broadcasted_iota(jnp.int32, (tk, 1), 0)
    cmp_end = lax.broadcasted_iota(jnp.int32, (n_cmp_pad, 1), 0) * CMP_STRIDE + (CMP_BLOCK - 1)
    cmp_mask = cmp_end <= tpos
    blk = lax.broadcasted_iota(jnp.int32, (n_sel, 1), 0)
    cur = tpos >> 6
    forced = (blk == 0) | (blk == cur) | (blk == cur - 1)
    win_lo = jnp.maximum(tpos - WINDOW, -1)

    def run_tiles(tiles, ahead=NSA_SCORES_AHEAD):
        chunks = []
        for sub in range(tq // tk):
            for branch, k_s, v_s, q_heads, h2, tile, mode in tiles:
                kt = jnp.maximum(tile, 0) * (tq // tk) + sub
                k0 = pl.multiple_of(kt * tk, tk)
                shared = dict(k=k_s[h2, pl.ds(k0, tk), :], v=v_s[h2, kt], mode=mode,
                              kpos=tile * tq + sub * tk + key_iota)
                for g in range(grp):
                    chunks.append(dict(shared, slot=(h2 * 2 + branch) * grp + g, q=q_heads[h2][g]))

        def scores(c):
            s = _dot(c["k"], c["q"])
            if c["mode"] == "causal":
                s = jnp.where(c["kpos"] <= tpos, s, NEG_INF)
            elif c["mode"] == "window_start":
                s = jnp.where(c["kpos"] > win_lo, s, NEG_INF)
            elif c["mode"] == "in_sequence":
                s = jnp.where(c["kpos"] >= 0, s, NEG_INF)
            c["s"] = s

        def softmax_update(c):
            m = m_s[c["slot"]]
            m_new = jnp.maximum(m, jnp.max(c["s"], axis=0, keepdims=True))
            c["p"] = jnp.exp2(c.pop("s") - m_new).astype(BF16)
            c["alpha"] = jnp.exp2(m - m_new)
            m_s[c["slot"]] = m_new

        def accumulate(c):
            acc_s[c["slot"]] = c["alpha"] * acc_s[c["slot"]] + _dot(c["v"], c.pop("p"))

        for c in chunks[:ahead]:
            scores(c)
        for i, c in enumerate(chunks):
            softmax_update(c)
            if i + ahead < len(chunks):
                scores(chunks[i + ahead])
            accumulate(c)

    def result(branch, h2, g):
        slot = (h2 * 2 + branch) * grp + g
        return acc_s[slot, 0:dh, :] / acc_s[slot, dh:dh + 1, :]

    m_s[...] = jnp.full(m_s.shape, NEG_INF, F32)
    acc_s[...] = jnp.zeros(acc_s.shape, F32)
    q_augs, o_cmps = [], []
    cmp_scores = [[_dot(kc_ref[0, h2, :, 0:dh], q_ref[0, (h2 * grp + g) * dh:(h2 * grp + g + 1) * dh, :])
                   for g in range(grp)] for h2 in range(2)]

    for h2 in range(2):
        head_rows = lambda ref, g: ref[0, (h2 * grp + g) * dh:(h2 * grp + g + 1) * dh, :]

        p_grp = jnp.zeros((n_cmp_pad, tq), F32)
        o_cmp = []
        for g in range(grp):
            s = jnp.where(cmp_mask, cmp_scores[h2][g], NEG_INF)
            e = jnp.where(cmp_mask, jnp.exp2(s - jnp.max(s, axis=0, keepdims=True)), 0.0)
            inv = 1.0 / jnp.maximum(jnp.sum(e, axis=0, keepdims=True), 1e-30)
            o_cmp.append(_dot(vcmp_s[h2], e.astype(BF16)) * inv)
            p_grp = p_grp + e * inv
        i3 = _dot(wt_ref[...], jnp.concatenate(_split3(p_grp), axis=1))
        imp = i3[:, 0:tq] + i3[:, tq:2 * tq] + i3[:, 2 * tq:]
        imp = jnp.where(blk > cur, -1.0, jnp.where(forced, FORCE_SCORE, imp))
        rows = 8
        ranks = []
        for r0 in range(0, n_sel, rows):
            mine = imp[r0:r0 + rows]
            row_id = blk[r0:r0 + rows]
            rank = jnp.zeros((rows, tq), F32)
            for jp in range(n_sel):
                other = imp[jp:jp + 1, :]
                ge = lambda: jnp.where(other >= mine, 1.0, 0.0)
                gt_ = lambda: jnp.where(other > mine, 1.0, 0.0)
                if jp < r0:
                    rank = rank + ge()
                elif jp >= r0 + rows:
                    rank = rank + gt_()
                else:
                    rank = rank + jnp.where(row_id > jp, ge(), gt_())
            ranks.append(rank)
        rank = jnp.concatenate(ranks, axis=0)
        sel_bias = jnp.where(rank < float(SEL_TOPK), 0.0, NEG_INF).astype(BF16)
        pad = jnp.zeros((dh - n_sel, tq), BF16)
        q_augs.append([jnp.concatenate([head_rows(qr_ref, g), sel_bias, pad], axis=0) for g in range(grp)])
        o_cmps.append(o_cmp)

    sel, win = 0, 1

    no_bias = jnp.zeros((dh, tq), BF16)
    q_wins = [[jnp.concatenate([qr_ref[0, (h2 * grp + g) * dh:(h2 * grp + g + 1) * dh, :], no_bias], axis=0)
               for g in range(grp)] for h2 in range(2)]
    run_tiles([(win, kwin_s, vwin_s, q_wins, h2, qi - 2, "window_start") for h2 in range(2)]
              + [(win, kwin_s, vwin_s, q_wins, h2, qi - 1, "in_sequence") for h2 in range(2)]
              + [(win, kwin_s, vwin_s, q_wins, h2, qi, "causal") for h2 in range(2)]
              + [(sel, ksel_s, vsel_s, q_augs, h2, qi, "causal") for h2 in range(2)],
              ahead=NSA_SCORES_AHEAD - 2)

    def sel_tiles(kts):
        return [(sel, ksel_s, vsel_s, q_augs, h2, kt, None) for kt in kts for h2 in range(2)]

    group = NSA_SEL_TILES_PER_TRIP

    def sel_body(trip, carry):
        run_tiles(sel_tiles([trip * group + j for j in range(group)]))
        return carry

    lax.fori_loop(0, qi // group, sel_body, 0)
    done = (qi // group) * group
    size = group // 2
    while size >= 1:
        @pl.when(((qi - done) & size) != 0)
        def _(done=done, size=size):
            run_tiles(sel_tiles([done + j for j in range(size)]))

        done = done + ((qi - done) & size)
        size //= 2

    for h2 in range(2):
        for g in range(grp):
            hg = h2 * grp + g
            gate = lambda branch: gt_ref[0, branch * heads + hg:branch * heads + hg + 1, :]
            o_ref[0, hg * dh:(hg + 1) * dh, :] = (
                gate(0) * o_cmps[h2][g] + gate(1) * result(sel, h2, g) + gate(2) * result(win, h2, g)
            ).astype(o_ref.dtype)


def _nsa_attention(q_t, qr_t, gates_t, kvc, kv):
    b, d, s = q_t.shape
    tq, tk = NSA_TQ, NSA_TK
    assert WINDOW == 2 * tq and tq % SEL_BLOCK == 0 and tq % tk == 0
    dh = NSA_DH
    pairs = NSA_KV_HEADS // 2
    heads = 2 * NSA_GROUP
    pw = heads * dh
    n_cmp_pad = kvc.shape[2]
    n_sel = s // SEL_BLOCK
    wt = jnp.asarray(_cmp_to_sel_weights_t(n_cmp_pad, n_sel), BF16)
    onehot = np.zeros((s, dh), np.float32)
    onehot[np.arange(s), np.arange(s) // SEL_BLOCK] = 1.0
    kv_spec = lambda grp: pl.BlockSpec((1, s, LANE), lambda bi, p, i: (bi, 0, pairs * grp + p))
    return pl.pallas_call(
        functools.partial(_nsa_kernel, tq=tq, tk=tk),
        out_shape=jax.ShapeDtypeStruct((b, d, s), BF16),
        grid=(b, pairs, s // tq),
        in_specs=[
            pl.BlockSpec((1, pw, tq), lambda bi, p, i: (bi, p, i)),
            pl.BlockSpec((1, pw, tq), lambda bi, p, i: (bi, p, i)),
            pl.BlockSpec((1, LANE, tq), lambda bi, p, i: (bi, p, i)),
            pl.BlockSpec((1, 2, n_cmp_pad, LANE), lambda bi, p, i: (bi, p, 0, 0)),
            kv_spec(0), kv_spec(1), kv_spec(2), kv_spec(3),
            _const_spec(wt.shape),
            _const_spec(onehot.shape),
        ],
        out_specs=pl.BlockSpec((1, pw, tq), lambda bi, p, i: (bi, p, i)),
        scratch_shapes=[
            pltpu.VMEM((2, s, 2 * dh), BF16),
            pltpu.VMEM((2, s, 2 * dh), BF16),
            pltpu.VMEM((2, s // tk, NSA_ACC_ROWS, tk), BF16),
            pltpu.VMEM((2, s // tk, NSA_ACC_ROWS, tk), BF16),
            pltpu.VMEM((2, dh, n_cmp_pad), BF16),
            pltpu.VMEM((2 * heads, 1, tq), F32),
            pltpu.VMEM((2 * heads, NSA_ACC_ROWS, tq), F32),
        ],
        compiler_params=_params("parallel", "parallel", "arbitrary"),
    )(q_t, qr_t, gates_t, kvc, kv, kv, kv, kv, wt, jnp.asarray(onehot, BF16))


def _rope_tables(seq, width):
    half = NSA_DH // 2
    inv = ROPE_THETA ** (-jnp.arange(half, dtype=F32) / half)
    ang = jnp.arange(seq, dtype=F32)[:, None] * inv[None, :]
    cos, sin = jnp.cos(ang), jnp.sin(ang)
    reps = width // NSA_DH
    cos_full = jnp.tile(jnp.concatenate([cos, cos], axis=1), (1, reps))
    sin_signed = jnp.tile(jnp.concatenate([-sin, sin], axis=1), (1, reps))
    return cos_full, sin_signed


def _nsa_gate_weights(w_gate):
    d = w_gate.shape[0]
    pairs = NSA_KV_HEADS // 2
    heads = 2 * NSA_GROUP
    wg = w_gate.reshape(d, 3, pairs, heads).transpose(0, 2, 1, 3).reshape(d, pairs, 3 * heads)
    return jnp.pad(wg, ((0, 0), (0, 0), (0, LANE - 3 * heads))).reshape(d, pairs * LANE)


def kernel(x, norm_mix, norm_ffn, gla_w_in, gla_w_alpha_up, gla_b_alpha, gla_norm, gla_w_o, kv_norm, nsa_w_kv, cmp_pe_k, cmp_pe_v, cmp_k_w1, cmp_k_w2, cmp_v_w1, cmp_v_w2, nsa_w_in, nsa_w_o, ffn_w_up, ffn_conv_w, ffn_conv_b, ffn_w_down, norm_final):
    b, s, d = x.shape
    n = b * s
    x2 = x.reshape(n, d)

    n_main = 2 * GLA_HEADS * GLA_DK + 2 * GLA_HEADS * GLA_DV
    w_in = gla_w_in[0]
    n_pairs = len(GATE_PIECE_PAIRS)
    lane_pad = LANE - n_pairs * GLA_GATE_RANK
    w_low = jnp.pad(jnp.tile(w_in[:, n_main:], (1, n_pairs)), ((0, 0), (0, lane_pad))).astype(BF16)
    w_up_pieces = _split3(gla_w_alpha_up[0])
    w_up6 = jnp.pad(jnp.concatenate([w_up_pieces[iw] for _, iw in GATE_PIECE_PAIRS], axis=0),
                    ((0, lane_pad), (0, 0)))
    proj, log_a = _gla_front(x2, norm_mix[0], w_in[:, :n_main].astype(BF16), w_low, w_up6, gla_b_alpha[0])
    o = _gla_core(proj.reshape(b, s, n_main), log_a.reshape(b, s, -1), gla_norm[0])
    x2 = _conv_ffn(x, o, gla_w_o[0].astype(BF16), norm_ffn[0], ffn_w_up[0].astype(BF16), ffn_conv_w[0],
                   ffn_conv_b[0], ffn_w_down[0].astype(BF16)).reshape(n, d)

    hk, dh = NSA_KV_HEADS, NSA_DH
    nq = NSA_HEADS * dh
    w_nsa = nsa_w_in[0]
    kv, k16, v16, q_t, qr_t, gates_t = _nsa_front(
        x2, kv_norm, norm_mix[1], nsa_w_kv.astype(BF16), w_nsa[:, :nq].astype(BF16),
        _nsa_gate_weights(w_nsa[:, nq:]).astype(BF16), _rope_tables(s, hk * dh), seq=s,
        q_scale=dh ** -0.5 * LOG2E)
    kvc = _compress(k16, v16, cmp_pe_k, cmp_pe_v, cmp_k_w1.astype(BF16), cmp_v_w1.astype(BF16),
                    cmp_k_w2.astype(BF16), cmp_v_w2.astype(BF16), rows_per_seq=s // CMP_STRIDE)

    o_t = _nsa_attention(q_t, qr_t, gates_t, kvc, kv.reshape(b, s, -1))
    return _conv_ffn(x2.reshape(b, s, d), o_t, nsa_w_o[0].astype(BF16), norm_ffn[1], ffn_w_up[1].astype(BF16),
                     ffn_conv_w[1], ffn_conv_b[1], ffn_w_down[1].astype(BF16), norm_final, mix_transposed=True)
```
